```python
import jax, jax.numpy as jnp
from jax import lax
import numpy as np

D_MODEL = 1024
BATCH = 4
SEQ = 4096
DEPTH = 2

EPS = 1e-6
NEG_INF = -1e30
HEAD_DIM = 64
ATT_HEADS = 8
ATT_WIDTH = ATT_HEADS * HEAD_DIM
MOBA_BLOCK = 256
MOBA_TOPK = 3
MOBA_Q_CHUNK = 32
ROPE_THETA = 10000.0
CONV_WIDTH = 256
CONV_GROUPS = 4
CONV_KERNEL = 31
GLA_HEADS = 4
GLA_DK = 64
GLA_DV = 64
GLA_KWIDTH = GLA_HEADS * GLA_DK
GLA_VWIDTH = GLA_HEADS * GLA_DV
GLA_GATE_RANK = 16
GLA_TAU = 16.0
GLA_CHUNK = 64
MIX_WIDTH = ATT_WIDTH + CONV_WIDTH + GLA_VWIDTH
IN_WIDTH = 3 * ATT_WIDTH + 2 * CONV_WIDTH + 2 * GLA_KWIDTH + 2 * GLA_VWIDTH + GLA_GATE_RANK
D_FF = 2816
FFN_RES = 0.5
N_MOD = 9

kernel_name = "hybrid_moba_conv_gla_macaron_adaln"


def _rmsnorm(x, gain):
    x32 = x.astype(jnp.float32)
    y = x32 * lax.rsqrt(jnp.mean(x32 * x32, axis=-1, keepdims=True) + EPS)
    return (y * gain.astype(jnp.float32)).astype(x.dtype)


def _layernorm(x, gain, bias):
    x32 = x.astype(jnp.float32)
    mu = jnp.mean(x32, axis=-1, keepdims=True)
    xc = x32 - mu
    y = xc * lax.rsqrt(jnp.mean(xc * xc, axis=-1, keepdims=True) + EPS)
    return (y * gain.astype(jnp.float32) + bias.astype(jnp.float32)).astype(x.dtype)


def _modulate(h, shift, scale):
    return h * (1 + scale[:, None, :]) + shift[:, None, :]


def _swiglu(h, w_in, w_out):
    a, b = jnp.split(h @ w_in, 2, axis=-1)
    return (jax.nn.silu(a) * b) @ w_out


def _split_heads(t, n_heads):
    bsz, seq, _ = t.shape
    return t.reshape(bsz, seq, n_heads, -1).transpose(0, 2, 1, 3)


def _merge_heads(t):
    bsz, n_h, seq, hd = t.shape
    return t.transpose(0, 2, 1, 3).reshape(bsz, seq, n_h * hd)


def _rope_tables(positions):
    inv_freq = 1.0 / (ROPE_THETA ** (jnp.arange(0, HEAD_DIM, 2, dtype=jnp.float32) / HEAD_DIM))
    ang = positions.astype(jnp.float32)[:, :, None] * inv_freq
    return jnp.cos(ang)[:, None], jnp.sin(ang)[:, None]


def _rope(t, cos, sin):
    t32 = t.astype(jnp.float32)
    t1, t2 = jnp.split(t32, 2, axis=-1)
    return jnp.concatenate([t1 * cos - t2 * sin, t2 * cos + t1 * sin], axis=-1).astype(t.dtype)


def _moba_attention(q, k, v):
    bsz, n_h, seq, hd = q.shape
    n_blk = -(-seq // MOBA_BLOCK)
    pad = n_blk * MOBA_BLOCK - seq
    widths = ((0, 0), (0, 0), (0, pad), (0, 0))
    kb = jnp.pad(k, widths).reshape(bsz, n_h, n_blk, MOBA_BLOCK, hd)
    vb = jnp.pad(v, widths).reshape(bsz, n_h, n_blk, MOBA_BLOCK, hd)
    k_mean = jnp.mean(kb.astype(jnp.float32), axis=3)
    gate = jnp.einsum('bhsd,bhnd->bhsn', q.astype(jnp.float32), k_mean)
    q_blk = jnp.arange(seq) // MOBA_BLOCK
    past = jnp.arange(n_blk)[None, :] < q_blk[:, None]
    gate = jnp.where(past, gate, NEG_INF)
    n_sel = min(MOBA_TOPK, n_blk)
    _, sel = lax.top_k(gate, n_sel)
    sel_ok = sel < q_blk[:, None]
    b_ix = jnp.arange(bsz)[:, None, None, None]
    h_ix = jnp.arange(n_h)[None, :, None, None]
    scale = hd ** -0.5
    qc_len = MOBA_Q_CHUNK

    def one_chunk(ci):
        t0 = ci * qc_len
        blk = t0 // MOBA_BLOCK
        qc = lax.dynamic_slice_in_dim(q, t0, qc_len, axis=2)
        sc = lax.dynamic_slice_in_dim(sel, t0, qc_len, axis=2)
        okc = lax.dynamic_slice_in_dim(sel_ok, t0, qc_len, axis=2)
        k_own = lax.dynamic_index_in_dim(kb, blk, axis=2, keepdims=False)
        v_own = lax.dynamic_index_in_dim(vb, blk, axis=2, keepdims=False)
        k_sel = kb[b_ix, h_ix, sc]
        v_sel = vb[b_ix, h_ix, sc]
        s_own = jnp.einsum('bhqd,bhld->bhql', qc, k_own).astype(jnp.float32) * scale
        q_pos = t0 + jnp.arange(qc_len)
        k_pos = blk * MOBA_BLOCK + jnp.arange(MOBA_BLOCK)
        s_own = jnp.where(k_pos[None, :] <= q_pos[:, None], s_own, NEG_INF)
        s_sel = jnp.einsum('bhqd,bhqnld->bhqnl', qc, k_sel).astype(jnp.float32) * scale
        s_sel = jnp.where(okc[..., None], s_sel, NEG_INF)
        s_all = jnp.concatenate([s_own, s_sel.reshape(bsz, n_h, qc_len, n_sel * MOBA_BLOCK)], axis=-1)
        p = jax.nn.softmax(s_all, axis=-1).astype(v.dtype)
        p_own = p[..., :MOBA_BLOCK]
        p_sel = p[..., MOBA_BLOCK:].reshape(bsz, n_h, qc_len, n_sel, MOBA_BLOCK)
        return (jnp.einsum('bhql,bhld->bhqd', p_own, v_own)
                + jnp.einsum('bhqnl,bhqnld->bhqd', p_sel, v_sel))

    out = lax.map(one_chunk, jnp.arange(seq // qc_len))
    return out.transpose(1, 2, 0, 3, 4).reshape(bsz, n_h, seq, hd)


def _conformer_conv(u, w_dw, b_dw, gain, bias):
    a, g = jnp.split(u, 2, axis=-1)
    h = a * jax.nn.sigmoid(g)
    h = lax.conv_general_dilated(h, w_dw[:, None, :], (1,), [(CONV_KERNEL - 1, 0)],
                                 dimension_numbers=('NWC', 'WIO', 'NWC'),
                                 feature_group_count=CONV_WIDTH) + b_dw
    bsz, seq, ch = h.shape
    hg = h.reshape(bsz, seq, CONV_GROUPS, ch // CONV_GROUPS)
    hg = _layernorm(hg, gain.reshape(CONV_GROUPS, -1), bias.reshape(CONV_GROUPS, -1))
    return jax.nn.silu(hg.reshape(bsz, seq, ch))


def _gla(q, k, v, g):
    bsz, n_h, seq, dk = q.shape
    dv = v.shape[-1]
    L = GLA_CHUNK
    n_c = seq // L
    f32 = jnp.float32
    qc = (q.astype(f32) * dk ** -0.5).reshape(bsz, n_h, n_c, L, dk)
    kc = k.astype(f32).reshape(bsz, n_h, n_c, L, dk)
    vc = v.astype(f32).reshape(bsz, n_h, n_c, L, dv)
    G = lax.cumsum(g.astype(f32).reshape(bsz, n_h, n_c, L, dk), axis=3)
    q_t = qc * jnp.exp(G)
    k_t = kc * jnp.exp(-G)
    causal = jnp.tril(jnp.ones((L, L), dtype=bool))
    A = jnp.where(causal, jnp.einsum('bhnld,bhnmd->bhnlm', q_t, k_t), 0.0)
    o_intra = jnp.einsum('bhnlm,bhnmv->bhnlv', A, vc)
    G_last = G[:, :, :, -1]
    k_end = kc * jnp.exp(G_last[:, :, :, None, :] - G)
    kv = jnp.einsum('bhnld,bhnlv->bhndv', k_end, vc)

    def step(state, inp):
        decay, kv_n = inp
        return decay[..., None] * state + kv_n, state

    init = jnp.zeros((bsz, n_h, dk, dv), f32)
    _, s_prev = lax.scan(step, init, (jnp.moveaxis(jnp.exp(G_last), 2, 0), jnp.moveaxis(kv, 2, 0)))
    s_prev = jnp.moveaxis(s_prev, 0, 2)
    o_inter = jnp.einsum('bhnld,bhndv->bhnlv', q_t, s_prev)
    return (o_intra + o_inter).reshape(bsz, n_h, seq, dv).astype(v.dtype)


def setup_inputs(seed: int = 0) -> dict:
    key = jax.random.key(seed)
    ks = jax.random.split(key, 24)
    f32 = jnp.float32

    def nrm(k, shape, s):
        return jax.random.normal(k, shape, f32) * s

    def gain(k, shape):
        return 1.0 + 0.05 * jax.random.normal(k, shape, f32)

    offset = jax.random.randint(ks[2], (BATCH, 1), 0, 1024, dtype=jnp.int32)
    positions = (offset + jnp.arange(SEQ, dtype=jnp.int32)[None, :]).astype(jnp.int32)
    return {
        "x": nrm(ks[0], (BATCH, SEQ, D_MODEL), 1.0),
        "c": nrm(ks[1], (BATCH, D_MODEL), 1.0),
        "positions": positions,
        "ada_w": nrm(ks[3], (DEPTH, D_MODEL, N_MOD * D_MODEL), 0.5 * D_MODEL ** -0.5),
        "ada_b": nrm(ks[4], (DEPTH, N_MOD * D_MODEL), 0.02),
        "ffn1_norm": gain(ks[5], (DEPTH, D_MODEL)),
        "ffn1_w_in": nrm(ks[6], (DEPTH, D_MODEL, 2 * D_FF), D_MODEL ** -0.5),
        "ffn1_w_out": nrm(ks[7], (DEPTH, D_FF, D_MODEL), D_FF ** -0.5),
        "mix_norm": gain(ks[8], (DEPTH, D_MODEL)),
        "mix_w_in": nrm(ks[9], (DEPTH, D_MODEL, IN_WIDTH), D_MODEL ** -0.5),
        "q_norm": gain(ks[10], (DEPTH, HEAD_DIM)),
        "k_norm": gain(ks[11], (DEPTH, HEAD_DIM)),
        "conv_w": nrm(ks[12], (DEPTH, CONV_KERNEL, CONV_WIDTH), CONV_KERNEL ** -0.5),
        "conv_b": nrm(ks[13], (DEPTH, CONV_WIDTH), 0.02),
        "conv_norm_g": gain(ks[14], (DEPTH, CONV_WIDTH)),
        "conv_norm_b": nrm(ks[15], (DEPTH, CONV_WIDTH), 0.02),
        "gla_gate_w": nrm(ks[16], (DEPTH, GLA_GATE_RANK, GLA_KWIDTH), GLA_GATE_RANK ** -0.5),
        "gla_gate_b": nrm(ks[17], (DEPTH, GLA_KWIDTH), 0.1),
        "gla_out_norm": gain(ks[18], (DEPTH, GLA_DV)),
        "mix_w_out": nrm(ks[19], (DEPTH, MIX_WIDTH, D_MODEL), MIX_WIDTH ** -0.5),
        "ffn2_norm": gain(ks[20], (DEPTH, D_MODEL)),
        "ffn2_w_in": nrm(ks[21], (DEPTH, D_MODEL, 2 * D_FF), D_MODEL ** -0.5),
        "ffn2_w_out": nrm(ks[22], (DEPTH, D_FF, D_MODEL), D_FF ** -0.5),
    }


def reference(x, c, positions, ada_w, ada_b, ffn1_norm, ffn1_w_in, ffn1_w_out, mix_norm, mix_w_in,
              q_norm, k_norm, conv_w, conv_b, conv_norm_g, conv_norm_b, gla_gate_w, gla_gate_b,
              gla_out_norm, mix_w_out, ffn2_norm, ffn2_w_in, ffn2_w_out):
    cos, sin = _rope_tables(positions)
    c_act = jax.nn.silu(c)
    splits = list(np.cumsum([ATT_WIDTH, ATT_WIDTH, ATT_WIDTH, 2 * CONV_WIDTH,
                             GLA_KWIDTH, GLA_KWIDTH, GLA_VWIDTH, GLA_VWIDTH]))
    for l in range(DEPTH):
        mod = c_act @ ada_w[l] + ada_b[l]
        sh1, sc1, gt1, sh2, sc2, gt2, sh3, sc3, gt3 = jnp.split(mod, N_MOD, axis=-1)

        h = _modulate(_rmsnorm(x, ffn1_norm[l]), sh1, sc1)
        x = x + FFN_RES * gt1[:, None, :] * _swiglu(h, ffn1_w_in[l], ffn1_w_out[l])

        h = _modulate(_rmsnorm(x, mix_norm[l]), sh2, sc2)
        proj = h @ mix_w_in[l]
        a_q, a_k, a_v, b_u, c_q, c_k, c_v, c_r, c_g = jnp.split(proj, splits, axis=-1)

        qa = _rope(_rmsnorm(_split_heads(a_q, ATT_HEADS), q_norm[l]), cos, sin)
        ka = _rope(_rmsnorm(_split_heads(a_k, ATT_HEADS), k_norm[l]), cos, sin)
        va = _split_heads(a_v, ATT_HEADS)
        o_a = _merge_heads(_moba_attention(qa, ka, va))

        o_b = _conformer_conv(b_u, conv_w[l], conv_b[l], conv_norm_g[l], conv_norm_b[l])

        log_decay = jax.nn.log_sigmoid(c_g @ gla_gate_w[l] + gla_gate_b[l]) / GLA_TAU
        o_c = _gla(_split_heads(c_q, GLA_HEADS), _split_heads(c_k, GLA_HEADS),
                   _split_heads(c_v, GLA_HEADS), _split_heads(log_decay, GLA_HEADS))
        o_c = _merge_heads(_rmsnorm(o_c, gla_out_norm[l])) * jax.nn.silu(c_r)

        mixed = jnp.concatenate([o_a, o_b, o_c], axis=-1) @ mix_w_out[l]
        x = x + gt2[:, None, :] * mixed

        h = _modulate(_rmsnorm(x, ffn2_norm[l]), sh3, sc3)
        x = x + FFN_RES * gt3[:, None, :] * _swiglu(h, ffn2_w_in[l], ffn2_w_out[l])
    return x
```

```python
import functools

import jax
import jax.numpy as jnp
from jax import lax
from jax.experimental import pallas as pl
from jax.experimental.pallas import tpu as pltpu

F32 = jnp.float32
BF16 = jnp.bfloat16

EPS = 1e-6
NEG_INF = -1e30
HEAD_DIM = 64
ATT_HEADS = 8
ATT_WIDTH = ATT_HEADS * HEAD_DIM
MOBA_BLOCK = 256
MOBA_TOPK = 3
ROPE_THETA = 10000.0
CONV_WIDTH = 256
CONV_GROUPS = 4
CONV_KERNEL = 31
GLA_HEADS = 4
GLA_WIDTH = 256
GLA_GATE_RANK = 16
GLA_TAU = 16.0
GLA_CHUNK = 64
FFN_RES = 0.5
N_MOD = 9

V7X_VMEM_BYTES = 64 * 2**20
LANES = 128
SUBLANES = 8
ROW_TILE = 512
FF_CHUNK = 256
SEQ_TILE = 512
CONV_HALO = 32
CONV_SUB = 128

_NT = (((1,), (1,)), ((), ()))
_TN = (((0,), (0,)), ((), ()))


def _cparams(semantics, vmem_bytes):
    assert vmem_bytes < V7X_VMEM_BYTES
    return pltpu.CompilerParams(dimension_semantics=semantics, vmem_limit_bytes=vmem_bytes)


def _resident(shape):
    zeros = (0,) * len(shape)
    return pl.BlockSpec(shape, lambda *_: zeros, pipeline_mode=pl.Buffered(1))


def _split_dot(x, w, terms, dims=None):
    acc, r = None, x
    for t in range(terms):
        p = r.astype(BF16)
        if dims is None:
            d = jnp.dot(p, w, preferred_element_type=F32)
        else:
            d = lax.dot_general(w, p, dims, preferred_element_type=F32)
        acc = d if acc is None else acc + d
        if t + 1 < terms:
            r = r - p.astype(F32)
    return acc


def _group_sum(x, bd):
    return _split_dot(x, bd, 2)


def _block_diag_ones(n, group):
    idx = jnp.arange(n) // group
    return (idx[:, None] == idx[None, :]).astype(BF16)


def _mod_norm(x, gain, scale, shift):
    y = x * lax.rsqrt(jnp.mean(x * x, axis=-1, keepdims=True) + EPS)
    return (y * gain) * (1.0 + scale) + shift


def _ada_body(c_ref, w_ref, b_ref, o_ref):
    c = c_ref[...]
    c_act = (c * jax.nn.sigmoid(c)).astype(BF16)
    o_ref[0] = jnp.dot(c_act, w_ref[0].astype(BF16), preferred_element_type=F32) + b_ref[0]


def _ada(c, ada_w, ada_b):
    depth, d, m = ada_w.shape
    b = c.shape[0]
    tn = 1024
    return pl.pallas_call(
        _ada_body,
        out_shape=jax.ShapeDtypeStruct((depth, b, m), F32),
        grid=(depth, m // tn),
        in_specs=[pl.BlockSpec((b, d), lambda l, j: (0, 0)),
                  pl.BlockSpec((1, d, tn), lambda l, j: (l, 0, j)),
                  pl.BlockSpec((1, 1, tn), lambda l, j: (l, 0, j))],
        out_specs=pl.BlockSpec((1, b, tn), lambda l, j: (l, 0, j)),
        compiler_params=_cparams(("parallel", "parallel"), 32 * 2**20),
        name="ada_mod",
    )(c, ada_w, ada_b.reshape(depth, 1, m))


def _rope_body(pos_ref, invf_ref, sign_ref, cos_ref, sin_ref):
    ang = pos_ref[0].astype(F32) * invf_ref[...]
    cos_ref[0] = jnp.cos(ang)
    sin_ref[0] = jnp.sin(ang) * sign_ref[...]


def _rope_tables(positions):
    b, s = positions.shape
    half = HEAD_DIM // 2
    inv_freq = 1.0 / (ROPE_THETA ** (jnp.arange(0, HEAD_DIM, 2, dtype=F32) / HEAD_DIM))
    invf = jnp.tile(inv_freq, LANES // half).reshape(1, LANES)
    sign = jnp.tile(jnp.concatenate([-jnp.ones((half,), F32), jnp.ones((half,), F32)]), LANES // HEAD_DIM).reshape(1, LANES)
    ts = min(s, 1024)
    shape = jax.ShapeDtypeStruct((b, s, LANES), F32)
    return pl.pallas_call(
        _rope_body,
        out_shape=(shape, shape),
        grid=(b, s // ts),
        in_specs=[pl.BlockSpec((1, ts, 1), lambda i, j: (i, j, 0)),
                  pl.BlockSpec((1, LANES), lambda i, j: (0, 0)),
                  pl.BlockSpec((1, LANES), lambda i, j: (0, 0))],
        out_specs=(pl.BlockSpec((1, ts, LANES), lambda i, j: (i, j, 0)),
                   pl.BlockSpec((1, ts, LANES), lambda i, j: (i, j, 0))),
        compiler_params=_cparams(("parallel", "parallel"), 32 * 2**20),
        name="rope_tables",
    )(positions.reshape(b, s, 1), invf, sign)


def _ffn_body(x_ref, mod_ref, gain_ref, wi_ref, wo_ref, o_ref, hb_ref, acc_ref, *, mod_row, n_chunks):
    x = x_ref[...]
    shift = mod_ref[0, mod_row:mod_row + 1, :]
    scale = mod_ref[0, mod_row + 1:mod_row + 2, :]
    gate = mod_ref[0, mod_row + 2:mod_row + 3, :]
    hb_ref[...] = _mod_norm(x, gain_ref[...], scale, shift).astype(BF16)
    acc_ref[...] = jnp.zeros_like(acc_ref)

    def step(c, carry):
        ab = jnp.dot(hb_ref[...], wi_ref[c], preferred_element_type=F32)
        a = ab[:, :FF_CHUNK]
        g = (a * jax.nn.sigmoid(a) * ab[:, FF_CHUNK:]).astype(BF16)
        acc_ref[...] += jnp.dot(g, wo_ref[c], preferred_element_type=F32)
        return carry

    lax.fori_loop(0, n_chunks, step, 0)
    o_ref[...] = x + (FFN_RES * gate) * acc_ref[...]


def _prep_ffn(w_in, w_out):
    d, two_f = w_in.shape
    f = two_f // 2
    n_chunks = f // FF_CHUNK
    a = w_in[:, :f].reshape(d, n_chunks, FF_CHUNK)
    b = w_in[:, f:].reshape(d, n_chunks, FF_CHUNK)
    wi = jnp.concatenate([a, b], axis=2).transpose(1, 0, 2).astype(BF16)
    wo = w_out.reshape(n_chunks, FF_CHUNK, d).astype(BF16)
    return wi, wo


def _ffn(x, mod, gain, wi, wo, *, mod_row, rows_per_batch):
    n, d = x.shape
    n_chunks = wi.shape[0]
    tm = min(ROW_TILE, rows_per_batch)
    tiles_per_batch = rows_per_batch // tm
    body = functools.partial(_ffn_body, mod_row=mod_row, n_chunks=n_chunks)
    return pl.pallas_call(
        body,
        out_shape=jax.ShapeDtypeStruct((n, d), F32),
        grid=(n // tm,),
        in_specs=[pl.BlockSpec((tm, d), lambda i: (i, 0)),
                  pl.BlockSpec((1, N_MOD, d), lambda i: (i // tiles_per_batch, 0, 0)),
                  pl.BlockSpec((1, d), lambda i: (0, 0)),
                  _resident(wi.shape),
                  _resident(wo.shape)],
        out_specs=pl.BlockSpec((tm, d), lambda i: (i, 0)),
        scratch_shapes=[pltpu.VMEM((tm, d), BF16), pltpu.VMEM((tm, d), F32)],
        compiler_params=_cparams(("parallel",), 48 * 2**20),
        name="ffn",
    )(x, mod, gain.reshape(1, d), wi, wo)


def _log_sigmoid(z):
    return jnp.minimum(z, 0.0) - jnp.log1p(jnp.exp(-jnp.abs(z)))


def _proj_body(x_ref, mod_ref, gain_ref, wm_ref, wg_ref, gw_ref, gb_ref, qg_ref, kg_ref, cos_ref, sin_ref, bd_ref,
               q_ref, k_ref, v_ref, u_ref, gq_ref, gk_ref, gv_ref, gr_ref, gd_ref, hb_ref):
    x = x_ref[...]
    hb_ref[...] = _mod_norm(x, gain_ref[...], mod_ref[0, 4:5, :], mod_ref[0, 3:4, :]).astype(BF16)

    def proj(lo, hi):
        return jnp.dot(hb_ref[...], wm_ref[:, lo:hi], preferred_element_type=F32)

    reps = ATT_WIDTH // LANES
    cos = jnp.concatenate([cos_ref[0]] * reps, axis=1)
    sin = jnp.concatenate([sin_ref[0]] * reps, axis=1)
    lane = lax.broadcasted_iota(jnp.int32, cos.shape, 1)
    first_half = (lane & (HEAD_DIM // 2)) == 0

    def norm_rope(t, g_ref):
        ss = _group_sum(t * t, bd_ref[...])
        y = (t * lax.rsqrt(ss * (1.0 / HEAD_DIM) + EPS)) * g_ref[...]
        partner = jnp.where(first_half, pltpu.roll(y, ATT_WIDTH - HEAD_DIM // 2, 1), pltpu.roll(y, HEAD_DIM // 2, 1))
        return y * cos + partner * sin

    w = ATT_WIDTH
    q_ref[...] = norm_rope(proj(0, w), qg_ref).astype(BF16)
    k_ref[...] = norm_rope(proj(w, 2 * w), kg_ref).astype(BF16)
    v_ref[...] = proj(2 * w, 3 * w).astype(BF16)
    o = 3 * w
    ug = proj(o, o + 2 * CONV_WIDTH)
    u_ref[...] = ug[:, :CONV_WIDTH] * jax.nn.sigmoid(ug[:, CONV_WIDTH:])
    o += 2 * CONV_WIDTH
    gq_ref[...] = proj(o, o + GLA_WIDTH)
    gk_ref[...] = proj(o + GLA_WIDTH, o + 2 * GLA_WIDTH)
    gv_ref[...] = proj(o + 2 * GLA_WIDTH, o + 3 * GLA_WIDTH)
    gr_ref[...] = proj(o + 3 * GLA_WIDTH, o + 4 * GLA_WIDTH)
    cg = jnp.dot(hb_ref[...], wg_ref[...], preferred_element_type=F32).astype(BF16)
    z = jnp.dot(cg, gw_ref[...], preferred_element_type=F32) + gb_ref[...]
    gd_ref[...] = _log_sigmoid(z) * (1.0 / GLA_TAU)


def _proj(x, mod, gain, w_in, gate_w, gate_b, q_gain, k_gain, cos, sin, *, rows_per_batch):
    n, d = x.shape
    main = 3 * ATT_WIDTH + 2 * CONV_WIDTH + 4 * GLA_WIDTH
    wm = w_in[:, :main].astype(BF16)
    wg = jnp.pad(w_in[:, main:], ((0, 0), (0, LANES - GLA_GATE_RANK))).astype(BF16)
    gw = jnp.pad(gate_w, ((0, LANES - GLA_GATE_RANK), (0, 0))).astype(BF16)
    bd = _block_diag_ones(ATT_WIDTH, HEAD_DIM)
    tm = min(ROW_TILE, rows_per_batch)
    tpb = rows_per_batch // tm
    row = lambda i: (i, 0)
    const = lambda i: (0, 0)
    tab = pl.BlockSpec((1, tm, LANES), lambda i: (i // tpb, i % tpb, 0))
    att = jax.ShapeDtypeStruct((n, ATT_WIDTH), BF16)
    g32 = jax.ShapeDtypeStruct((n, GLA_WIDTH), F32)
    return pl.pallas_call(
        _proj_body,
        out_shape=(att, att, att, jax.ShapeDtypeStruct((n, CONV_WIDTH), F32), g32, g32, g32, g32, g32),
        grid=(n // tm,),
        in_specs=[pl.BlockSpec((tm, d), row),
                  pl.BlockSpec((1, N_MOD, d), lambda i: (i // tpb, 0, 0)),
                  pl.BlockSpec((1, d), const),
                  _resident(wm.shape), _resident(wg.shape), _resident(gw.shape),
                  pl.BlockSpec((1, GLA_WIDTH), const),
                  pl.BlockSpec((1, ATT_WIDTH), const), pl.BlockSpec((1, ATT_WIDTH), const),
                  tab, tab, _resident(bd.shape)],
        out_specs=(pl.BlockSpec((tm, ATT_WIDTH), row),) * 3 + (pl.BlockSpec((tm, CONV_WIDTH), row),)
        + (pl.BlockSpec((tm, GLA_WIDTH), row),) * 5,
        scratch_shapes=[pltpu.VMEM((tm, d), BF16)],
        compiler_params=_cparams(("parallel",), 48 * 2**20),
        name="mix_proj",
    )(x, mod, gain.reshape(1, d), wm, wg, gw, gate_b.reshape(1, GLA_WIDTH),
      jnp.tile(q_gain, ATT_HEADS).reshape(1, ATT_WIDTH), jnp.tile(k_gain, ATT_HEADS).reshape(1, ATT_WIDTH),
      cos, sin, bd)


def _attn_body(q_ref, k_ref, v_ref, o_ref, vt_ref, bias_ref, s_ref, ot_ref, *, n_blk):
    blk = MOBA_BLOCK
    seq = n_blk * blk
    heads = LANES // HEAD_DIM
    lane = lax.broadcasted_iota(jnp.int32, (1, LANES), 1)
    log2_blk = blk.bit_length() - 1
    log2_hd = HEAD_DIM.bit_length() - 1

    def vt_step(j, carry):
        r0 = pl.multiple_of(j * blk, blk)
        vt_ref[j] = v_ref[0, pl.ds(r0, blk), :].astype(F32).T.astype(BF16)
        return carry

    lax.fori_loop(0, n_blk, vt_step, 0)

    kmean = jnp.concatenate(
        [jnp.mean(k_ref[0, j * blk:(j + 1) * blk, :].astype(F32), axis=0, keepdims=True) for j in range(n_blk)], axis=0)

    jrow = lax.broadcasted_iota(jnp.int32, (n_blk, seq), 0)
    tcol = lax.broadcasted_iota(jnp.int32, (n_blk, seq), 1)
    past = jrow < lax.shift_right_logical(tcol, log2_blk)
    for h in range(heads):
        in_head = lax.shift_right_logical(lane, log2_hd) == h
        kmh = jnp.where(in_head, kmean, 0.0).astype(BF16)
        g = lax.dot_general(kmh, q_ref[0], _NT, preferred_element_type=F32)
        g = jnp.where(past, g, NEG_INF)
        rank = jnp.zeros((n_blk, seq), F32)
        for jp in range(n_blk):
            gj = g[jp:jp + 1, :]
            rank = rank + jnp.where(jrow > jp, jnp.where(gj >= g, 1.0, 0.0), jnp.where(gj > g, 1.0, 0.0))
        sel = jnp.logical_and(rank < float(min(MOBA_TOPK, n_blk)), past)
        bias = jnp.where(sel, 0.0, NEG_INF)
        for i in range(n_blk):
            bi = bias[:, i * blk:(i + 1) * blk]
            bias_ref[h, i] = jnp.broadcast_to(bi[:, None, :], (n_blk, SUBLANES, blk)).reshape(n_blk * SUBLANES, blk)

    key_row = lax.broadcasted_iota(jnp.int32, (blk, blk), 0)
    qry_col = lax.broadcasted_iota(jnp.int32, (blk, blk), 1)
    causal = key_row <= qry_col
    scale = HEAD_DIM ** -0.5
    for h in range(heads):
        in_head = lax.shift_right_logical(lane, log2_hd) == h
        v_rows = slice(h * HEAD_DIM, (h + 1) * HEAD_DIM)

        def q_block(i, carry, h=h, in_head=in_head, v_rows=v_rows):
            q0 = pl.multiple_of(i * blk, blk)
            qi = q_ref[0, pl.ds(q0, blk), :]
            qh = (jnp.where(in_head, qi, jnp.zeros_like(qi)).astype(F32) * scale).astype(BF16)

            def scores(j, m8):
                k0 = pl.multiple_of(j * blk, blk)
                s = lax.dot_general(k_ref[0, pl.ds(k0, blk), :], qh, _NT, preferred_element_type=F32)
                b8 = bias_ref[h, i, pl.ds(pl.multiple_of(j * SUBLANES, SUBLANES), SUBLANES), :]
                s3 = s.reshape(blk // SUBLANES, SUBLANES, blk) + b8[None]
                s_ref[j] = s3.reshape(blk, blk)
                return jnp.maximum(m8, jnp.max(s3, axis=0))

            m8 = lax.fori_loop(0, i, scores, jnp.full((SUBLANES, blk), -jnp.inf, F32))
            s_own = lax.dot_general(k_ref[0, pl.ds(q0, blk), :], qh, _NT, preferred_element_type=F32)
            s_own = jnp.where(causal, s_own, NEG_INF)
            s_ref[i] = s_own
            m8 = jnp.maximum(m8, jnp.max(s_own.reshape(blk // SUBLANES, SUBLANES, blk), axis=0))
            m = jnp.max(m8, axis=0, keepdims=True)

            def weighted(j, lc):
                l8, acc = lc
                p = jnp.exp(s_ref[j] - m)
                l8 = l8 + jnp.sum(p.reshape(blk // SUBLANES, SUBLANES, blk), axis=0)
                acc = acc + jnp.dot(vt_ref[j, v_rows, :], p.astype(BF16), preferred_element_type=F32)
                return l8, acc

            l8, acc = lax.fori_loop(0, i + 1, weighted,
                                    (jnp.zeros((SUBLANES, blk), F32), jnp.zeros((HEAD_DIM, blk), F32)))
            ot_ref[i, v_rows, :] = acc / jnp.sum(l8, axis=0, keepdims=True)
            return carry

        lax.fori_loop(0, n_blk, q_block, 0)

    def out_step(i, carry):
        r0 = pl.multiple_of(i * blk, blk)
        o_ref[0, pl.ds(r0, blk), :] = ot_ref[i].T.astype(BF16)
        return carry

    lax.fori_loop(0, n_blk, out_step, 0)


def _attn(q, k, v):
    b, s, w = q.shape
    assert s % MOBA_BLOCK == 0
    n_blk = s // MOBA_BLOCK
    heads = LANES // HEAD_DIM
    spec = pl.BlockSpec((1, s, LANES), lambda i, p: (i, 0, p))
    return pl.pallas_call(
        functools.partial(_attn_body, n_blk=n_blk),
        out_shape=jax.ShapeDtypeStruct((b, s, w), BF16),
        grid=(b, w // LANES),
        in_specs=[spec, spec, spec],
        out_specs=spec,
        scratch_shapes=[pltpu.VMEM((n_blk, LANES, MOBA_BLOCK), BF16),
                        pltpu.VMEM((heads, n_blk, n_blk * SUBLANES, MOBA_BLOCK), F32),
                        pltpu.VMEM((n_blk, MOBA_BLOCK, MOBA_BLOCK), F32),
                        pltpu.VMEM((n_blk, LANES, MOBA_BLOCK), F32)],
        compiler_params=_cparams(("parallel", "parallel"), 48 * 2**20),
        name="moba_attn",
    )(q, k, v)


def _conv_body(cur_ref, prev_ref, w_ref, cb_ref, g_ref, b_ref, bd_ref, o_ref, win_ref, *, tile):
    halo = prev_ref[0, tile - CONV_HALO:tile, :]
    win_ref[0:CONV_HALO] = jnp.where(pl.program_id(1) > 0, halo, 0.0)
    win_ref[CONV_HALO:CONV_HALO + tile] = cur_ref[0]
    lead = CONV_HALO - (CONV_KERNEL - 1)
    inv = 1.0 / (CONV_WIDTH // CONV_GROUPS)
    for r0 in range(0, tile, CONV_SUB):
        acc = jnp.broadcast_to(cb_ref[...], (CONV_SUB, CONV_WIDTH))
        for t in range(CONV_KERNEL):
            acc = acc + w_ref[t:t + 1, :] * win_ref[r0 + lead + t:r0 + lead + t + CONV_SUB, :]
        mu = _group_sum(acc, bd_ref[...]) * inv
        xc = acc - mu
        var = _group_sum(xc * xc, bd_ref[...]) * inv
        y = (xc * lax.rsqrt(var + EPS)) * g_ref[...] + b_ref[...]
        o_ref[0, r0:r0 + CONV_SUB, :] = (y * jax.nn.sigmoid(y)).astype(BF16)


def _conv(u, w_dw, b_dw, gain, bias):
    b, s, c = u.shape
    tile = min(SEQ_TILE, s)
    w = jnp.pad(w_dw, ((0, CONV_HALO - CONV_KERNEL), (0, 0)))
    bd = _block_diag_ones(c, c // CONV_GROUPS)
    vec = pl.BlockSpec((1, c), lambda i, j: (0, 0))
    return pl.pallas_call(
        functools.partial(_conv_body, tile=tile),
        out_shape=jax.ShapeDtypeStruct((b, s, c), BF16),
        grid=(b, s // tile),
        in_specs=[pl.BlockSpec((1, tile, c), lambda i, j: (i, j, 0)),
                  pl.BlockSpec((1, tile, c), lambda i, j: (i, jnp.maximum(j - 1, 0), 0)),
                  pl.BlockSpec((CONV_HALO, c), lambda i, j: (0, 0)),
                  vec, vec, vec, pl.BlockSpec((c, c), lambda i, j: (0, 0))],
        out_specs=pl.BlockSpec((1, tile, c), lambda i, j: (i, j, 0)),
        scratch_shapes=[pltpu.VMEM((CONV_HALO + tile, c), F32)],
        compiler_params=_cparams(("parallel", "parallel"), 32 * 2**20),
        name="conv_module",
    )(u, u, w, b_dw.reshape(1, c), gain.reshape(1, c), bias.reshape(1, c), bd)


def _gla_body(q_ref, k_ref, v_ref, g_ref, r_ref, gain_ref, bd_ref, tri_ref, o_ref, st_ref, *, tile):
    @pl.when(pl.program_id(1) == 0)
    def _():
        st_ref[...] = jnp.zeros_like(st_ref)

    w = GLA_WIDTH
    ch = GLA_CHUNK
    log2_hd = HEAD_DIM.bit_length() - 1
    lane = lax.broadcasted_iota(jnp.int32, (ch, w), 1)
    row = lax.broadcasted_iota(jnp.int32, (ch, w), 0)
    head_of_lane = lax.shift_right_logical(lane, log2_hd)
    causal = (lane & (HEAD_DIM - 1)) <= row
    r_bd = lax.shift_right_logical(lax.broadcasted_iota(jnp.int32, (w, w), 0), log2_hd)
    c_bd = lax.shift_right_logical(lax.broadcasted_iota(jnp.int32, (w, w), 1), log2_hd)
    same_head = r_bd == c_bd
    scale = HEAD_DIM ** -0.5

    def stack_heads(t):
        return jnp.concatenate([jnp.where(head_of_lane == h, t, 0.0) for h in range(GLA_HEADS)], axis=0).astype(BF16)

    for c0 in range(0, tile, ch):
        rows = slice(c0, c0 + ch)
        q = q_ref[0, rows, :]
        k = k_ref[0, rows, :]
        v = v_ref[0, rows, :]
        cum = _split_dot(g_ref[0, rows, :], tri_ref[...], 3, dims=(((1,), (0,)), ((), ())))
        last = cum[ch - 1:ch, :]
        q_t = ((q * scale) * jnp.exp(cum)).astype(BF16)
        k_t = k * jnp.exp(-cum)
        k_end = (k * jnp.exp(last - cum)).astype(BF16)
        decay = jnp.exp(last)
        a = lax.dot_general(q_t, stack_heads(k_t), _NT, preferred_element_type=F32)
        a = jnp.where(causal, a, 0.0).astype(BF16)
        o_intra = jnp.dot(a, stack_heads(v), preferred_element_type=F32)
        st = st_ref[...]
        o_inter = lax.dot_general(q_t, st.astype(BF16), _NT, preferred_element_type=F32)
        kv_t = lax.dot_general(v.astype(BF16), k_end, _TN, preferred_element_type=F32)
        st_ref[...] = st * decay + jnp.where(same_head, kv_t, 0.0)
        o = o_intra + o_inter
        ss = _group_sum(o * o, bd_ref[...])
        y = (o * lax.rsqrt(ss * (1.0 / HEAD_DIM) + EPS)) * gain_ref[...]
        r = r_ref[0, rows, :]
        o_ref[0, rows, :] = (y * (r * jax.nn.sigmoid(r))).astype(BF16)


def _gla(q, k, v, g, r, out_gain):
    b, s, w = q.shape
    tile = min(SEQ_TILE, s)
    bd = _block_diag_ones(w, HEAD_DIM)
    idx = jnp.arange(GLA_CHUNK)
    tri = (idx[None, :] <= idx[:, None]).astype(BF16)
    seq = pl.BlockSpec((1, tile, w), lambda i, j: (i, j, 0))
    return pl.pallas_call(
        functools.partial(_gla_body, tile=tile),
        out_shape=jax.ShapeDtypeStruct((b, s, w), BF16),
        grid=(b, s // tile),
        in_specs=[seq, seq, seq, seq, seq,
                  pl.BlockSpec((1, w), lambda i, j: (0, 0)),
                  pl.BlockSpec((w, w), lambda i, j: (0, 0)),
                  pl.BlockSpec((GLA_CHUNK, GLA_CHUNK), lambda i, j: (0, 0))],
        out_specs=seq,
        scratch_shapes=[pltpu.VMEM((w, w), F32)],
        compiler_params=_cparams(("parallel", "arbitrary"), 32 * 2**20),
        name="gla",
    )(q, k, v, g, r, jnp.tile(out_gain, GLA_HEADS).reshape(1, w), bd, tri)


def _out_body(x_ref, mod_ref, oa_ref, ob_ref, oc_ref, w_ref, o_ref):
    wa = ATT_WIDTH
    wb = wa + CONV_WIDTH
    acc = jnp.dot(oa_ref[...], w_ref[0:wa, :], preferred_element_type=F32)
    acc = acc + jnp.dot(ob_ref[...], w_ref[wa:wb, :], preferred_element_type=F32)
    acc = acc + jnp.dot(oc_ref[...], w_ref[wb:wb + GLA_WIDTH, :], preferred_element_type=F32)
    o_ref[...] = x_ref[...] + mod_ref[0, 5:6, :] * acc


def _out_proj(x, mod, o_a, o_b, o_c, w_out, *, rows_per_batch):
    n, d = x.shape
    tm = min(ROW_TILE, rows_per_batch)
    tpb = rows_per_batch // tm
    row = lambda i: (i, 0)
    wb = w_out.astype(BF16)
    return pl.pallas_call(
        _out_body,
        out_shape=jax.ShapeDtypeStruct((n, d), F32),
        grid=(n // tm,),
        in_specs=[pl.BlockSpec((tm, d), row),
                  pl.BlockSpec((1, N_MOD, d), lambda i: (i // tpb, 0, 0)),
                  pl.BlockSpec((tm, ATT_WIDTH), row),
                  pl.BlockSpec((tm, CONV_WIDTH), row),
                  pl.BlockSpec((tm, GLA_WIDTH), row),
                  _resident(wb.shape)],
        out_specs=pl.BlockSpec((tm, d), row),
        compiler_params=_cparams(("parallel",), 32 * 2**20),
        name="mix_out",
    )(x, mod, o_a, o_b, o_c, wb)


def kernel(x, c, positions, ada_w, ada_b, ffn1_norm, ffn1_w_in, ffn1_w_out, mix_norm, mix_w_in, q_norm, k_norm, conv_w, conv_b, conv_norm_g, conv_norm_b, gla_gate_w, gla_gate_b, gla_out_norm, mix_w_out, ffn2_norm, ffn2_w_in, ffn2_w_out):
    b, s, d = x.shape
    depth = ada_w.shape[0]
    n = b * s
    mod = _ada(c, ada_w, ada_b).reshape(depth, b, N_MOD, d)
    cos, sin = _rope_tables(positions)
    xf = x.reshape(n, d)
    for l in range(depth):
        wi, wo = _prep_ffn(ffn1_w_in[l], ffn1_w_out[l])
        xf = _ffn(xf, mod[l], ffn1_norm[l], wi, wo, mod_row=0, rows_per_batch=s)
        q, k, v, u, gq, gk, gv, gr, gd = _proj(xf, mod[l], mix_norm[l], mix_w_in[l], gla_gate_w[l], gla_gate_b[l],
                                               q_norm[l], k_norm[l], cos, sin, rows_per_batch=s)
        o_a = _attn(q.reshape(b, s, ATT_WIDTH), k.reshape(b, s, ATT_WIDTH), v.reshape(b, s, ATT_WIDTH))
        o_b = _conv(u.reshape(b, s, CONV_WIDTH), conv_w[l], conv_b[l], conv_norm_g[l], conv_norm_b[l])
        three = lambda t: t.reshape(b, s, GLA_WIDTH)
        o_c = _gla(three(gq), three(gk), three(gv), three(gd), three(gr), gla_out_norm[l])
        xf = _out_proj(xf, mod[l], o_a.reshape(n, ATT_WIDTH), o_b.reshape(n, CONV_WIDTH), o_c.reshape(n, GLA_WIDTH),
                       mix_w_out[l], rows_per_batch=s)
        wi, wo = _prep_ffn(ffn2_w_in[l], ffn2_w_out[l])
        xf = _ffn(xf, mod[l], ffn2_norm[l], wi, wo, mod_row=6, rows_per_batch=s)
    return xf.reshape(b, s, d)
```

```python
import functools

import jax
import jax.numpy as jnp
from jax import lax
from jax.experimental import pallas as pl
from jax.experimental.pallas import tpu as pltpu

F32 = jnp.float32
BF16 = jnp.bfloat16

EPS = 1e-6
NEG_INF = -1e30
HEAD_DIM = 64
ATT_HEADS = 8
ATT_WIDTH = ATT_HEADS * HEAD_DIM
MOBA_BLOCK = 256
MOBA_TOPK = 3
ROPE_THETA = 10000.0
CONV_WIDTH = 256
CONV_GROUPS = 4
CONV_KERNEL = 31
GLA_HEADS = 4
GLA_WIDTH = 256
GLA_GATE_RANK = 16
GLA_TAU = 16.0
GLA_CHUNK = 64
FFN_RES = 0.5
N_MOD = 9

V7X_VMEM_BYTES = 64 * 2**20
LANES = 128
SUBLANES = 8
ROW_TILE = 512
FF_CHUNK = 256
SEQ_TILE = 512
CONV_HALO = 32
CONV_SUB = 128

_NT = (((1,), (1,)), ((), ()))
_TN = (((0,), (0,)), ((), ()))


def _cparams(semantics, vmem_bytes, flags=None):
    assert vmem_bytes < V7X_VMEM_BYTES
    return pltpu.CompilerParams(dimension_semantics=semantics, vmem_limit_bytes=vmem_bytes, flags=flags)


def _resident(shape):
    zeros = (0,) * len(shape)
    return pl.BlockSpec(shape, lambda *_: zeros, pipeline_mode=pl.Buffered(1))


def _split_dot(x, w, terms, dims=None):
    acc, r = None, x
    for t in range(terms):
        p = r.astype(BF16)
        if dims is None:
            d = jnp.dot(p, w, preferred_element_type=F32)
        else:
            d = lax.dot_general(w, p, dims, preferred_element_type=F32)
        acc = d if acc is None else acc + d
        if t + 1 < terms:
            r = r - p.astype(F32)
    return acc


def _group_sum(x, bd):
    return _split_dot(x, bd, 2)


def _block_diag_ones(n, group):
    idx = jnp.arange(n) // group
    return (idx[:, None] == idx[None, :]).astype(BF16)


def _mod_norm(x, gain, scale, shift):
    y = x * lax.rsqrt(jnp.mean(x * x, axis=-1, keepdims=True) + EPS)
    return (y * gain) * (1.0 + scale) + shift


def _ada_body(c_ref, w_ref, b_ref, o_ref):
    c = c_ref[...]
    c_act = (c * jax.nn.sigmoid(c)).astype(BF16)
    o_ref[0] = jnp.dot(c_act, w_ref[0].astype(BF16), preferred_element_type=F32) + b_ref[0]


def _ada(c, ada_w, ada_b):
    depth, d, m = ada_w.shape
    b = c.shape[0]
    tn = 1024
    return pl.pallas_call(
        _ada_body,
        out_shape=jax.ShapeDtypeStruct((depth, b, m), F32),
        grid=(depth, m // tn),
        in_specs=[pl.BlockSpec((b, d), lambda l, j: (0, 0)),
                  pl.BlockSpec((1, d, tn), lambda l, j: (l, 0, j)),
                  pl.BlockSpec((1, 1, tn), lambda l, j: (l, 0, j))],
        out_specs=pl.BlockSpec((1, b, tn), lambda l, j: (l, 0, j)),
        compiler_params=_cparams(("parallel", "parallel"), 32 * 2**20),
        name="ada_mod",
    )(c, ada_w, ada_b.reshape(depth, 1, m))


def _rope_body(pos_ref, invf_ref, sign_ref, cos_ref, sin_ref):
    ang = pos_ref[0].astype(F32) * invf_ref[...]
    cos_ref[0] = jnp.cos(ang)
    sin_ref[0] = jnp.sin(ang) * sign_ref[...]


def _rope_tables(positions):
    b, s = positions.shape
    half = HEAD_DIM // 2
    inv_freq = 1.0 / (ROPE_THETA ** (jnp.arange(0, HEAD_DIM, 2, dtype=F32) / HEAD_DIM))
    invf = jnp.tile(inv_freq, LANES // half).reshape(1, LANES)
    sign = jnp.tile(jnp.concatenate([-jnp.ones((half,), F32), jnp.ones((half,), F32)]), LANES // HEAD_DIM).reshape(1, LANES)
    ts = min(s, 1024)
    shape = jax.ShapeDtypeStruct((b, s, LANES), F32)
    return pl.pallas_call(
        _rope_body,
        out_shape=(shape, shape),
        grid=(b, s // ts),
        in_specs=[pl.BlockSpec((1, ts, 1), lambda i, j: (i, j, 0)),
                  pl.BlockSpec((1, LANES), lambda i, j: (0, 0)),
                  pl.BlockSpec((1, LANES), lambda i, j: (0, 0))],
        out_specs=(pl.BlockSpec((1, ts, LANES), lambda i, j: (i, j, 0)),
                   pl.BlockSpec((1, ts, LANES), lambda i, j: (i, j, 0))),
        compiler_params=_cparams(("parallel", "parallel"), 32 * 2**20),
        name="rope_tables",
    )(positions.reshape(b, s, 1), invf, sign)


def _ffn_body(x_ref, mod_ref, gain_ref, wi_ref, wo_ref, o_ref, hb_ref, acc_ref, *, mod_row, n_chunks):
    x = x_ref[...]
    shift = mod_ref[0, mod_row:mod_row + 1, :]
    scale = mod_ref[0, mod_row + 1:mod_row + 2, :]
    gate = mod_ref[0, mod_row + 2:mod_row + 3, :]
    hb_ref[...] = _mod_norm(x, gain_ref[...], scale, shift).astype(BF16)
    acc_ref[...] = jnp.zeros_like(acc_ref)

    def step(c, carry):
        ab = jnp.dot(hb_ref[...], wi_ref[c], preferred_element_type=F32)
        a = ab[:, :FF_CHUNK]
        g = (a * jax.nn.sigmoid(a) * ab[:, FF_CHUNK:]).astype(BF16)
        acc_ref[...] += jnp.dot(g, wo_ref[c], preferred_element_type=F32)
        return carry

    lax.fori_loop(0, n_chunks, step, 0)
    o_ref[...] = x + (FFN_RES * gate) * acc_ref[...]


def _prep_ffn(w_in, w_out):
    d, two_f = w_in.shape
    f = two_f // 2
    n_chunks = f // FF_CHUNK
    a = w_in[:, :f].reshape(d, n_chunks, FF_CHUNK)
    b = w_in[:, f:].reshape(d, n_chunks, FF_CHUNK)
    wi = jnp.concatenate([a, b], axis=2).transpose(1, 0, 2).astype(BF16)
    wo = w_out.reshape(n_chunks, FF_CHUNK, d).astype(BF16)
    return wi, wo


def _ffn(x, mod, gain, wi, wo, *, mod_row, rows_per_batch):
    n, d = x.shape
    n_chunks = wi.shape[0]
    tm = min(ROW_TILE, rows_per_batch)
    tiles_per_batch = rows_per_batch // tm
    body = functools.partial(_ffn_body, mod_row=mod_row, n_chunks=n_chunks)
    return pl.pallas_call(
        body,
        out_shape=jax.ShapeDtypeStruct((n, d), F32),
        grid=(n // tm,),
        in_specs=[pl.BlockSpec((tm, d), lambda i: (i, 0)),
                  pl.BlockSpec((1, N_MOD, d), lambda i: (i // tiles_per_batch, 0, 0)),
                  pl.BlockSpec((1, d), lambda i: (0, 0)),
                  _resident(wi.shape),
                  _resident(wo.shape)],
        out_specs=pl.BlockSpec((tm, d), lambda i: (i, 0)),
        scratch_shapes=[pltpu.VMEM((tm, d), BF16), pltpu.VMEM((tm, d), F32)],
        compiler_params=_cparams(("parallel",), 48 * 2**20),
        name="ffn",
    )(x, mod, gain.reshape(1, d), wi, wo)


def _log_sigmoid(z):
    return jnp.minimum(z, 0.0) - jnp.log1p(jnp.exp(-jnp.abs(z)))


def _proj_body(x_ref, mod_ref, gain_ref, wm_ref, wg_ref, gw_ref, gb_ref, qg_ref, kg_ref, cos_ref, sin_ref, bd_ref,
               q_ref, k_ref, v_ref, u_ref, gq_ref, gk_ref, gv_ref, gr_ref, gd_ref, hb_ref):
    x = x_ref[...]
    hb_ref[...] = _mod_norm(x, gain_ref[...], mod_ref[0, 4:5, :], mod_ref[0, 3:4, :]).astype(BF16)

    def proj(lo, hi):
        return jnp.dot(hb_ref[...], wm_ref[:, lo:hi], preferred_element_type=F32)

    reps = ATT_WIDTH // LANES
    cos = jnp.concatenate([cos_ref[0]] * reps, axis=1)
    sin = jnp.concatenate([sin_ref[0]] * reps, axis=1)
    lane = lax.broadcasted_iota(jnp.int32, cos.shape, 1)
    first_half = (lane & (HEAD_DIM // 2)) == 0

    def norm_rope(t, g_ref):
        ss = _group_sum(t * t, bd_ref[...])
        y = (t * lax.rsqrt(ss * (1.0 / HEAD_DIM) + EPS)) * g_ref[...]
        partner = jnp.where(first_half, pltpu.roll(y, ATT_WIDTH - HEAD_DIM // 2, 1), pltpu.roll(y, HEAD_DIM // 2, 1))
        return y * cos + partner * sin

    w = ATT_WIDTH
    q_ref[...] = norm_rope(proj(0, w), qg_ref).astype(BF16)
    k_ref[...] = norm_rope(proj(w, 2 * w), kg_ref).astype(BF16)
    v_ref[...] = proj(2 * w, 3 * w).astype(BF16)
    o = 3 * w
    ug = proj(o, o + 2 * CONV_WIDTH)
    u_ref[...] = ug[:, :CONV_WIDTH] * jax.nn.sigmoid(ug[:, CONV_WIDTH:])
    o += 2 * CONV_WIDTH
    gq_ref[...] = proj(o, o + GLA_WIDTH)
    gk_ref[...] = proj(o + GLA_WIDTH, o + 2 * GLA_WIDTH)
    gv_ref[...] = proj(o + 2 * GLA_WIDTH, o + 3 * GLA_WIDTH)
    gr_ref[...] = proj(o + 3 * GLA_WIDTH, o + 4 * GLA_WIDTH)
    cg = jnp.dot(hb_ref[...], wg_ref[...], preferred_element_type=F32).astype(BF16)
    z = jnp.dot(cg, gw_ref[...], preferred_element_type=F32) + gb_ref[...]
    gd_ref[...] = _log_sigmoid(z) * (1.0 / GLA_TAU)


def _proj(x, mod, gain, w_in, gate_w, gate_b, q_gain, k_gain, cos, sin, *, rows_per_batch):
    n, d = x.shape
    main = 3 * ATT_WIDTH + 2 * CONV_WIDTH + 4 * GLA_WIDTH
    wm = w_in[:, :main].astype(BF16)
    wg = jnp.pad(w_in[:, main:], ((0, 0), (0, LANES - GLA_GATE_RANK))).astype(BF16)
    gw = jnp.pad(gate_w, ((0, LANES - GLA_GATE_RANK), (0, 0))).astype(BF16)
    bd = _block_diag_ones(ATT_WIDTH, HEAD_DIM)
    tm = min(ROW_TILE, rows_per_batch)
    tpb = rows_per_batch // tm
    row = lambda i: (i, 0)
    const = lambda i: (0, 0)
    tab = pl.BlockSpec((1, tm, LANES), lambda i: (i // tpb, i % tpb, 0))
    att = jax.ShapeDtypeStruct((n, ATT_WIDTH), BF16)
    g32 = jax.ShapeDtypeStruct((n, GLA_WIDTH), F32)
    return pl.pallas_call(
        _proj_body,
        out_shape=(att, att, att, jax.ShapeDtypeStruct((n, CONV_WIDTH), F32), g32, g32, g32, g32, g32),
        grid=(n // tm,),
        in_specs=[pl.BlockSpec((tm, d), row),
                  pl.BlockSpec((1, N_MOD, d), lambda i: (i // tpb, 0, 0)),
                  pl.BlockSpec((1, d), const),
                  _resident(wm.shape), _resident(wg.shape), _resident(gw.shape),
                  pl.BlockSpec((1, GLA_WIDTH), const),
                  pl.BlockSpec((1, ATT_WIDTH), const), pl.BlockSpec((1, ATT_WIDTH), const),
                  tab, tab, _resident(bd.shape)],
        out_specs=(pl.BlockSpec((tm, ATT_WIDTH), row),) * 3 + (pl.BlockSpec((tm, CONV_WIDTH), row),)
        + (pl.BlockSpec((tm, GLA_WIDTH), row),) * 5,
        scratch_shapes=[pltpu.VMEM((tm, d), BF16)],
        compiler_params=_cparams(("parallel",), 48 * 2**20),
        name="mix_proj",
    )(x, mod, gain.reshape(1, d), wm, wg, gw, gate_b.reshape(1, GLA_WIDTH),
      jnp.tile(q_gain, ATT_HEADS).reshape(1, ATT_WIDTH), jnp.tile(k_gain, ATT_HEADS).reshape(1, ATT_WIDTH),
      cos, sin, bd)


def _attn_body(q_ref, k_ref, v_ref, o_ref, vt_ref, bias_ref, s_ref, qq_ref, ot_ref, *, n_blk):
    blk = MOBA_BLOCK
    seq = n_blk * blk
    heads = LANES // HEAD_DIM
    lane = lax.broadcasted_iota(jnp.int32, (1, LANES), 1)
    log2_blk = blk.bit_length() - 1
    log2_hd = HEAD_DIM.bit_length() - 1

    def vt_step(j, carry):
        r0 = pl.multiple_of(j * blk, blk)
        vt_ref[j] = v_ref[0, pl.ds(r0, blk), :].astype(F32).T.astype(BF16)
        return carry

    lax.fori_loop(0, n_blk, vt_step, 0)

    kmean = jnp.concatenate(
        [jnp.mean(k_ref[0, j * blk:(j + 1) * blk, :].astype(F32), axis=0, keepdims=True) for j in range(n_blk)], axis=0)

    jrow = lax.broadcasted_iota(jnp.int32, (n_blk, seq), 0)
    tcol = lax.broadcasted_iota(jnp.int32, (n_blk, seq), 1)
    past = jrow < lax.shift_right_logical(tcol, log2_blk)
    for h in range(heads):
        in_head = lax.shift_right_logical(lane, log2_hd) == h
        kmh = jnp.where(in_head, kmean, 0.0).astype(BF16)
        g = lax.dot_general(kmh, q_ref[0], _NT, preferred_element_type=F32)
        g = jnp.where(past, g, NEG_INF)
        rank = jnp.zeros((n_blk, seq), F32)
        for jp in range(n_blk):
            gj = g[jp:jp + 1, :]
            rank = rank + jnp.where(jrow > jp, jnp.where(gj >= g, 1.0, 0.0), jnp.where(gj > g, 1.0, 0.0))
        sel = jnp.logical_and(rank < float(min(MOBA_TOPK, n_blk)), past)
        bias = jnp.where(sel, 0.0, NEG_INF)
        for i in range(n_blk):
            bi = bias[:, i * blk:(i + 1) * blk]
            bias_ref[h, i] = jnp.broadcast_to(bi[:, None, :], (n_blk, SUBLANES, blk)).reshape(n_blk * SUBLANES, blk)

    key_row = lax.broadcasted_iota(jnp.int32, (blk, blk), 0)
    qry_col = lax.broadcasted_iota(jnp.int32, (blk, blk), 1)
    causal = key_row <= qry_col
    scale = HEAD_DIM ** -0.5
    n_past = n_blk - 1
    zero_v = jnp.zeros((HEAD_DIM, blk), BF16)

    def k_blk(j):
        return k_ref[0, pl.ds(pl.multiple_of(j * blk, blk), blk), :]

    def tile_max(s):
        return jnp.max(s.reshape(blk // SUBLANES, SUBLANES, blk), axis=0)

    def tile_sum(s):
        return jnp.sum(s.reshape(blk // SUBLANES, SUBLANES, blk), axis=0)

    def q_pair(t, carry):
        for h in range(heads):
            in_head = lax.shift_right_logical(lane, log2_hd) == h
            v_rows = slice(h * HEAD_DIM, (h + 1) * HEAD_DIM)
            ia = t
            ib = n_past - t

            def load_q(i, in_head=in_head):
                qi = q_ref[0, pl.ds(pl.multiple_of(i * blk, blk), blk), :]
                return (jnp.where(in_head, qi, jnp.zeros_like(qi)).astype(F32) * scale).astype(BF16)

            qa = load_q(ia)
            qb = load_q(ib)
            qq_ref[h, 0] = qa
            qq_ref[h, 1] = qb
            sa = jnp.where(causal, lax.dot_general(k_blk(ia), qa, _NT, preferred_element_type=F32), NEG_INF)
            sb = jnp.where(causal, lax.dot_general(k_blk(ib), qb, _NT, preferred_element_type=F32), NEG_INF)
            s_ref[h, 0] = sa
            s_ref[h, 1] = sb
            m8a = tile_max(sa)
            m8b = tile_max(sb)
            for m in range(n_past):
                is_a = m < t
                j = jnp.where(is_a, m, m - t)
                s = lax.dot_general(k_blk(j), qq_ref[h, jnp.where(is_a, 0, 1)], _NT, preferred_element_type=F32)
                b8 = bias_ref[h, jnp.where(is_a, ia, ib), pl.ds(pl.multiple_of(j * SUBLANES, SUBLANES), SUBLANES), :]
                s3 = s.reshape(blk // SUBLANES, SUBLANES, blk) + b8[None]
                s_ref[h, 2 + m] = s3.reshape(blk, blk)
                mx = jnp.max(s3, axis=0)
                m8a = jnp.where(is_a, jnp.maximum(m8a, mx), m8a)
                m8b = jnp.where(is_a, m8b, jnp.maximum(m8b, mx))
            ma = jnp.max(m8a, axis=0, keepdims=True)
            mb = jnp.max(m8b, axis=0, keepdims=True)

            def vt_blk(j, v_rows=v_rows):
                return vt_ref[j, v_rows, :]

            pa = jnp.exp(s_ref[h, 0] - ma)
            pb = jnp.exp(s_ref[h, 1] - mb)
            l8a = tile_sum(pa)
            l8b = tile_sum(pb)
            acc = jnp.dot(jnp.concatenate([vt_blk(ia), zero_v], axis=0), pa.astype(BF16), preferred_element_type=F32)
            acc = acc + jnp.dot(jnp.concatenate([zero_v, vt_blk(ib)], axis=0), pb.astype(BF16),
                                preferred_element_type=F32)
            for m in range(n_past):
                is_a = m < t
                j = jnp.where(is_a, m, m - t)
                p = jnp.exp(s_ref[h, 2 + m] - jnp.where(is_a, ma, mb))
                ps = tile_sum(p)
                l8a = l8a + jnp.where(is_a, ps, 0.0)
                l8b = l8b + jnp.where(is_a, 0.0, ps)
                vt = vt_blk(j)
                lhs = jnp.concatenate([jnp.where(is_a, vt, zero_v), jnp.where(is_a, zero_v, vt)], axis=0)
                acc = acc + jnp.dot(lhs, p.astype(BF16), preferred_element_type=F32)
            ot_ref[ia, v_rows, :] = acc[:HEAD_DIM] / jnp.sum(l8a, axis=0, keepdims=True)
            ot_ref[ib, v_rows, :] = acc[HEAD_DIM:] / jnp.sum(l8b, axis=0, keepdims=True)
        return carry

    lax.fori_loop(0, n_blk // 2, q_pair, 0)

    def out_step(i, carry):
        r0 = pl.multiple_of(i * blk, blk)
        o_ref[0, pl.ds(r0, blk), :] = ot_ref[i].T.astype(BF16)
        return carry

    lax.fori_loop(0, n_blk, out_step, 0)


def _attn(q, k, v):
    b, s, w = q.shape
    assert s % (2 * MOBA_BLOCK) == 0
    n_blk = s // MOBA_BLOCK
    heads = LANES // HEAD_DIM
    spec = pl.BlockSpec((1, s, LANES), lambda i, p: (i, 0, p))
    return pl.pallas_call(
        functools.partial(_attn_body, n_blk=n_blk),
        out_shape=jax.ShapeDtypeStruct((b, s, w), BF16),
        grid=(b, w // LANES),
        in_specs=[spec, spec, spec],
        out_specs=spec,
        scratch_shapes=[pltpu.VMEM((n_blk, LANES, MOBA_BLOCK), BF16),
                        pltpu.VMEM((heads, n_blk, n_blk * SUBLANES, MOBA_BLOCK), F32),
                        pltpu.VMEM((heads, n_blk + 1, MOBA_BLOCK, MOBA_BLOCK), F32),
                        pltpu.VMEM((heads, 2, MOBA_BLOCK, LANES), BF16),
                        pltpu.VMEM((n_blk, LANES, MOBA_BLOCK), F32)],
        compiler_params=_cparams(("parallel", "parallel"), 48 * 2**20),
        name="moba_attn",
    )(q, k, v)


def _conv_body(cur_ref, prev_ref, w_ref, cb_ref, g_ref, b_ref, bd_ref, o_ref, win_ref, *, tile):
    halo = prev_ref[0, tile - CONV_HALO:tile, :]
    win_ref[0:CONV_HALO] = jnp.where(pl.program_id(1) > 0, halo, 0.0)
    win_ref[CONV_HALO:CONV_HALO + tile] = cur_ref[0]
    lead = CONV_HALO - (CONV_KERNEL - 1)
    inv = 1.0 / (CONV_WIDTH // CONV_GROUPS)
    for r0 in range(0, tile, CONV_SUB):
        acc = jnp.broadcast_to(cb_ref[...], (CONV_SUB, CONV_WIDTH))
        for t in range(CONV_KERNEL):
            acc = acc + w_ref[t:t + 1, :] * win_ref[r0 + lead + t:r0 + lead + t + CONV_SUB, :]
        mu = _group_sum(acc, bd_ref[...]) * inv
        xc = acc - mu
        var = _group_sum(xc * xc, bd_ref[...]) * inv
        y = (xc * lax.rsqrt(var + EPS)) * g_ref[...] + b_ref[...]
        o_ref[0, r0:r0 + CONV_SUB, :] = (y * jax.nn.sigmoid(y)).astype(BF16)


def _conv(u, w_dw, b_dw, gain, bias):
    b, s, c = u.shape
    tile = min(SEQ_TILE, s)
    w = jnp.pad(w_dw, ((0, CONV_HALO - CONV_KERNEL), (0, 0)))
    bd = _block_diag_ones(c, c // CONV_GROUPS)
    vec = pl.BlockSpec((1, c), lambda i, j: (0, 0))
    return pl.pallas_call(
        functools.partial(_conv_body, tile=tile),
        out_shape=jax.ShapeDtypeStruct((b, s, c), BF16),
        grid=(b, s // tile),
        in_specs=[pl.BlockSpec((1, tile, c), lambda i, j: (i, j, 0)),
                  pl.BlockSpec((1, tile, c), lambda i, j: (i, jnp.maximum(j - 1, 0), 0)),
                  pl.BlockSpec((CONV_HALO, c), lambda i, j: (0, 0)),
                  vec, vec, vec, pl.BlockSpec((c, c), lambda i, j: (0, 0))],
        out_specs=pl.BlockSpec((1, tile, c), lambda i, j: (i, j, 0)),
        scratch_shapes=[pltpu.VMEM((CONV_HALO + tile, c), F32)],
        compiler_params=_cparams(("parallel", "parallel"), 32 * 2**20),
        name="conv_module",
    )(u, u, w, b_dw.reshape(1, c), gain.reshape(1, c), bias.reshape(1, c), bd)


def _gla_body(q_ref, k_ref, v_ref, g_ref, r_ref, gain_ref, bd_ref, tri_ref, o_ref, st_ref, *, tile):
    @pl.when(pl.program_id(1) == 0)
    def _():
        st_ref[...] = jnp.zeros_like(st_ref)

    w = GLA_WIDTH
    ch = GLA_CHUNK
    log2_hd = HEAD_DIM.bit_length() - 1
    lane = lax.broadcasted_iota(jnp.int32, (ch, w), 1)
    row = lax.broadcasted_iota(jnp.int32, (ch, w), 0)
    head_of_lane = lax.shift_right_logical(lane, log2_hd)
    causal = (lane & (HEAD_DIM - 1)) <= row
    r_bd = lax.shift_right_logical(lax.broadcasted_iota(jnp.int32, (w, w), 0), log2_hd)
    c_bd = lax.shift_right_logical(lax.broadcasted_iota(jnp.int32, (w, w), 1), log2_hd)
    same_head = r_bd == c_bd
    scale = HEAD_DIM ** -0.5

    def stack_heads(t):
        return jnp.concatenate([jnp.where(head_of_lane == h, t, 0.0) for h in range(GLA_HEADS)], axis=0).astype(BF16)

    for c0 in range(0, tile, ch):
        rows = slice(c0, c0 + ch)
        q = q_ref[0, rows, :]
        k = k_ref[0, rows, :]
        v = v_ref[0, rows, :]
        cum = _split_dot(g_ref[0, rows, :], tri_ref[...], 3, dims=(((1,), (0,)), ((), ())))
        last = cum[ch - 1:ch, :]
        q_t = ((q * scale) * jnp.exp(cum)).astype(BF16)
        k_t = k * jnp.exp(-cum)
        k_end = (k * jnp.exp(last - cum)).astype(BF16)
        decay = jnp.exp(last)
        a = lax.dot_general(q_t, stack_heads(k_t), _NT, preferred_element_type=F32)
        a = jnp.where(causal, a, 0.0).astype(BF16)
        o_intra = jnp.dot(a, stack_heads(v), preferred_element_type=F32)
        st = st_ref[...]
        o_inter = lax.dot_general(q_t, st.astype(BF16), _NT, preferred_element_type=F32)
        kv_t = lax.dot_general(v.astype(BF16), k_end, _TN, preferred_element_type=F32)
        st_ref[...] = st * decay + jnp.where(same_head, kv_t, 0.0)
        o = o_intra + o_inter
        ss = _group_sum(o * o, bd_ref[...])
        y = (o * lax.rsqrt(ss * (1.0 / HEAD_DIM) + EPS)) * gain_ref[...]
        r = r_ref[0, rows, :]
        o_ref[0, rows, :] = (y * (r * jax.nn.sigmoid(r))).astype(BF16)


def _gla(q, k, v, g, r, out_gain):
    b, s, w = q.shape
    tile = min(SEQ_TILE, s)
    bd = _block_diag_ones(w, HEAD_DIM)
    idx = jnp.arange(GLA_CHUNK)
    tri = (idx[None, :] <= idx[:, None]).astype(BF16)
    seq = pl.BlockSpec((1, tile, w), lambda i, j: (i, j, 0))
    return pl.pallas_call(
        functools.partial(_gla_body, tile=tile),
        out_shape=jax.ShapeDtypeStruct((b, s, w), BF16),
        grid=(b, s // tile),
        in_specs=[seq, seq, seq, seq, seq,
                  pl.BlockSpec((1, w), lambda i, j: (0, 0)),
                  pl.BlockSpec((w, w), lambda i, j: (0, 0)),
                  pl.BlockSpec((GLA_CHUNK, GLA_CHUNK), lambda i, j: (0, 0))],
        out_specs=seq,
        scratch_shapes=[pltpu.VMEM((w, w), F32)],
        compiler_params=_cparams(("parallel", "arbitrary"), 32 * 2**20),
        name="gla",
    )(q, k, v, g, r, jnp.tile(out_gain, GLA_HEADS).reshape(1, w), bd, tri)


def _out_body(x_ref, mod_ref, oa_ref, ob_ref, oc_ref, w_ref, o_ref):
    wa = ATT_WIDTH
    wb = wa + CONV_WIDTH
    acc = jnp.dot(oa_ref[...], w_ref[0:wa, :], preferred_element_type=F32)
    acc = acc + jnp.dot(ob_ref[...], w_ref[wa:wb, :], preferred_element_type=F32)
    acc = acc + jnp.dot(oc_ref[...], w_ref[wb:wb + GLA_WIDTH, :], preferred_element_type=F32)
    o_ref[...] = x_ref[...] + mod_ref[0, 5:6, :] * acc


def _out_proj(x, mod, o_a, o_b, o_c, w_out, *, rows_per_batch):
    n, d = x.shape
    tm = min(ROW_TILE, rows_per_batch)
    tpb = rows_per_batch // tm
    row = lambda i: (i, 0)
    wb = w_out.astype(BF16)
    return pl.pallas_call(
        _out_body,
        out_shape=jax.ShapeDtypeStruct((n, d), F32),
        grid=(n // tm,),
        in_specs=[pl.BlockSpec((tm, d), row),
                  pl.BlockSpec((1, N_MOD, d), lambda i: (i // tpb, 0, 0)),
                  pl.BlockSpec((tm, ATT_WIDTH), row),
                  pl.BlockSpec((tm, CONV_WIDTH), row),
                  pl.BlockSpec((tm, GLA_WIDTH), row),
                  _resident(wb.shape)],
        out_specs=pl.BlockSpec((tm, d), row),
        compiler_params=_cparams(("parallel",), 32 * 2**20),
        name="mix_out",
    )(x, mod, o_a, o_b, o_c, wb)


def kernel(x, c, positions, ada_w, ada_b, ffn1_norm, ffn1_w_in, ffn1_w_out, mix_norm, mix_w_in, q_norm, k_norm, conv_w, conv_b, conv_norm_g, conv_norm_b, gla_gate_w, gla_gate_b, gla_out_norm, mix_w_out, ffn2_norm, ffn2_w_in, ffn2_w_out):
    b, s, d = x.shape
    depth = ada_w.shape[0]
    n = b * s
    mod = _ada(c, ada_w, ada_b).reshape(depth, b, N_MOD, d)
    cos, sin = _rope_tables(positions)
    xf = x.reshape(n, d)
    for l in range(depth):
        wi, wo = _prep_ffn(ffn1_w_in[l], ffn1_w_out[l])
        xf = _ffn(xf, mod[l], ffn1_norm[l], wi, wo, mod_row=0, rows_per_batch=s)
        q, k, v, u, gq, gk, gv, gr, gd = _proj(xf, mod[l], mix_norm[l], mix_w_in[l], gla_gate_w[l], gla_gate_b[l],
                                               q_norm[l], k_norm[l], cos, sin, rows_per_batch=s)
        o_a = _attn(q.reshape(b, s, ATT_WIDTH), k.reshape(b, s, ATT_WIDTH), v.reshape(b, s, ATT_WIDTH))
        o_b = _conv(u.reshape(b, s, CONV_WIDTH), conv_w[l], conv_b[l], conv_norm_g[l], conv_norm_b[l])
        three = lambda t: t.reshape(b, s, GLA_WIDTH)
        o_c = _gla(three(gq), three(gk), three(gv), three(gd), three(gr), gla_out_norm[l])
        xf = _out_proj(xf, mod[l], o_a.reshape(n, ATT_WIDTH), o_b.reshape(n, CONV_WIDTH), o_c.reshape(n, GLA_WIDTH),
                       mix_w_out[l], rows_per_batch=s)
        wi, wo = _prep_ffn(ffn2_w_in[l], ffn2_w_out[l])
        xf = _ffn(xf, mod[l], ffn2_norm[l], wi, wo, mod_row=6, rows_per_batch=s)
    return xf.reshape(b, s, d)
```

```python
import functools

import jax
import jax.numpy as jnp
from jax import lax
from jax.experimental import pallas as pl
from jax.experimental.pallas import tpu as pltpu

F32 = jnp.float32
BF16 = jnp.bfloat16

EPS = 1e-6
NEG_INF = -1e30
HEAD_DIM = 64
ATT_HEADS = 8
ATT_WIDTH = ATT_HEADS * HEAD_DIM
MOBA_BLOCK = 256
MOBA_TOPK = 3
ROPE_THETA = 10000.0
CONV_WIDTH = 256
CONV_GROUPS = 4
CONV_KERNEL = 31
GLA_HEADS = 4
GLA_WIDTH = 256
GLA_GATE_RANK = 16
GLA_TAU = 16.0
GLA_CHUNK = 64
FFN_RES = 0.5
N_MOD = 9

V7X_VMEM_BYTES = 64 * 2**20
LANES = 128
SUBLANES = 8
BF16_ROWS = 16
LOG2E = 1.4426950408889634
ROW_TILE = 512
FF_CHUNK = 256
SEQ_TILE = 512
CONV_HALO = 32
CONV_SUB = 128

_NT = (((1,), (1,)), ((), ()))
_TN = (((0,), (0,)), ((), ()))


def _cparams(semantics, vmem_bytes, flags=None):
    assert vmem_bytes < V7X_VMEM_BYTES
    return pltpu.CompilerParams(dimension_semantics=semantics, vmem_limit_bytes=vmem_bytes, flags=flags)


def _resident(shape):
    zeros = (0,) * len(shape)
    return pl.BlockSpec(shape, lambda *_: zeros, pipeline_mode=pl.Buffered(1))


def _split_dot(x, w, terms, dims=None):
    acc, r = None, x
    for t in range(terms):
        p = r.astype(BF16)
        if dims is None:
            d = jnp.dot(p, w, preferred_element_type=F32)
        else:
            d = lax.dot_general(w, p, dims, preferred_element_type=F32)
        acc = d if acc is None else acc + d
        if t + 1 < terms:
            r = r - p.astype(F32)
    return acc


def _group_sum(x, bd):
    return _split_dot(x, bd, 2)


def _block_diag_ones(n, group):
    idx = jnp.arange(n) // group
    return (idx[:, None] == idx[None, :]).astype(BF16)


def _mod_norm(x, gain, scale, shift):
    y = x * lax.rsqrt(jnp.mean(x * x, axis=-1, keepdims=True) + EPS)
    return (y * gain) * (1.0 + scale) + shift


def _ada_body(c_ref, w_ref, b_ref, o_ref):
    c = c_ref[...]
    c_act = (c * jax.nn.sigmoid(c)).astype(BF16)
    o_ref[0] = jnp.dot(c_act, w_ref[0].astype(BF16), preferred_element_type=F32) + b_ref[0]


def _ada(c, ada_w, ada_b):
    depth, d, m = ada_w.shape
    b = c.shape[0]
    tn = 1024
    return pl.pallas_call(
        _ada_body,
        out_shape=jax.ShapeDtypeStruct((depth, b, m), F32),
        grid=(depth, m // tn),
        in_specs=[pl.BlockSpec((b, d), lambda l, j: (0, 0)),
                  pl.BlockSpec((1, d, tn), lambda l, j: (l, 0, j)),
                  pl.BlockSpec((1, 1, tn), lambda l, j: (l, 0, j))],
        out_specs=pl.BlockSpec((1, b, tn), lambda l, j: (l, 0, j)),
        compiler_params=_cparams(("parallel", "parallel"), 32 * 2**20),
        name="ada_mod",
    )(c, ada_w, ada_b.reshape(depth, 1, m))


def _rope_body(pos_ref, invf_ref, sign_ref, cos_ref, sin_ref):
    ang = pos_ref[0].astype(F32) * invf_ref[...]
    cos_ref[0] = jnp.cos(ang)
    sin_ref[0] = jnp.sin(ang) * sign_ref[...]


def _rope_tables(positions):
    b, s = positions.shape
    half = HEAD_DIM // 2
    inv_freq = 1.0 / (ROPE_THETA ** (jnp.arange(0, HEAD_DIM, 2, dtype=F32) / HEAD_DIM))
    invf = jnp.tile(inv_freq, LANES // half).reshape(1, LANES)
    sign = jnp.tile(jnp.concatenate([-jnp.ones((half,), F32), jnp.ones((half,), F32)]), LANES // HEAD_DIM).reshape(1, LANES)
    ts = min(s, 1024)
    shape = jax.ShapeDtypeStruct((b, s, LANES), F32)
    return pl.pallas_call(
        _rope_body,
        out_shape=(shape, shape),
        grid=(b, s // ts),
        in_specs=[pl.BlockSpec((1, ts, 1), lambda i, j: (i, j, 0)),
                  pl.BlockSpec((1, LANES), lambda i, j: (0, 0)),
                  pl.BlockSpec((1, LANES), lambda i, j: (0, 0))],
        out_specs=(pl.BlockSpec((1, ts, LANES), lambda i, j: (i, j, 0)),
                   pl.BlockSpec((1, ts, LANES), lambda i, j: (i, j, 0))),
        compiler_params=_cparams(("parallel", "parallel"), 32 * 2**20),
        name="rope_tables",
    )(positions.reshape(b, s, 1), invf, sign)


def _ffn_body(*refs, mod_row, d_ff, with_mix):
    if with_mix:
        x_ref, mod_ref, gain_ref, wi_ref, wo_ref, oa_ref, ob_ref, oc_ref, wm_ref, o_ref, hb_ref, acc_ref = refs
        wa = ATT_WIDTH
        wb = wa + CONV_WIDTH
        mixed = jnp.dot(oa_ref[...], wm_ref[0:wa, :], preferred_element_type=F32)
        mixed = mixed + jnp.dot(ob_ref[...], wm_ref[wa:wb, :], preferred_element_type=F32)
        mixed = mixed + jnp.dot(oc_ref[...], wm_ref[wb:wb + GLA_WIDTH, :], preferred_element_type=F32)
        x = x_ref[...] + mod_ref[0, 5:6, :] * mixed
    else:
        x_ref, mod_ref, gain_ref, wi_ref, wo_ref, o_ref, hb_ref, acc_ref = refs
        x = x_ref[...]
    shift = mod_ref[0, mod_row:mod_row + 1, :]
    scale = mod_ref[0, mod_row + 1:mod_row + 2, :]
    gate = mod_ref[0, mod_row + 2:mod_row + 3, :]
    hb_ref[...] = _mod_norm(x, gain_ref[...], scale, shift).astype(BF16)
    for c0 in range(0, d_ff, FF_CHUNK):
        a = jnp.dot(hb_ref[...], wi_ref[:, c0:c0 + FF_CHUNK], preferred_element_type=F32)
        b = jnp.dot(hb_ref[...], wi_ref[:, d_ff + c0:d_ff + c0 + FF_CHUNK], preferred_element_type=F32)
        g = (a * jax.nn.sigmoid(a) * b).astype(BF16)
        part = jnp.dot(g, wo_ref[c0:c0 + FF_CHUNK, :], preferred_element_type=F32)
        if c0 == 0:
            acc_ref[...] = part
        else:
            acc_ref[...] += part
    o_ref[...] = x + (FFN_RES * gate) * acc_ref[...]


def _ffn(x, mod, gain, w_in, w_out, *, mod_row, rows_per_batch, mix=None):
    n, d = x.shape
    d_ff = w_out.shape[0]
    assert d_ff % FF_CHUNK == 0
    tm = min(ROW_TILE, rows_per_batch)
    tiles_per_batch = rows_per_batch // tm
    row = lambda i: (i, 0)
    wi = w_in.astype(BF16)
    wo = w_out.astype(BF16)
    args = [x, mod, gain.reshape(1, d), wi, wo]
    in_specs = [pl.BlockSpec((tm, d), row),
                pl.BlockSpec((1, N_MOD, d), lambda i: (i // tiles_per_batch, 0, 0)),
                pl.BlockSpec((1, d), lambda i: (0, 0)),
                _resident(wi.shape),
                _resident(wo.shape)]
    if mix is not None:
        o_a, o_b, o_c, w_mix = mix
        wm = w_mix.astype(BF16)
        args += [o_a, o_b, o_c, wm]
        in_specs += [pl.BlockSpec((tm, ATT_WIDTH), row), pl.BlockSpec((tm, CONV_WIDTH), row),
                     pl.BlockSpec((tm, GLA_WIDTH), row), _resident(wm.shape)]
    body = functools.partial(_ffn_body, mod_row=mod_row, d_ff=d_ff, with_mix=mix is not None)
    return pl.pallas_call(
        body,
        out_shape=jax.ShapeDtypeStruct((n, d), F32),
        grid=(n // tm,),
        in_specs=in_specs,
        out_specs=pl.BlockSpec((tm, d), row),
        scratch_shapes=[pltpu.VMEM((tm, d), BF16), pltpu.VMEM((tm, d), F32)],
        compiler_params=_cparams(("parallel",), 48 * 2**20),
        name="ffn_mix" if mix is not None else "ffn",
    )(*args)


def _log_sigmoid(z):
    return jnp.minimum(z, 0.0) - jnp.log1p(jnp.exp(-jnp.abs(z)))


def _proj_body(x_ref, mod_ref, gain_ref, wm_ref, wg_ref, gw_ref, gb_ref, qg_ref, kg_ref, cos_ref, sin_ref, bd_ref,
               q_ref, k_ref, v_ref, u_ref, gq_ref, gk_ref, gv_ref, gr_ref, gd_ref, hb_ref):
    x = x_ref[...]
    hb_ref[...] = _mod_norm(x, gain_ref[...], mod_ref[0, 4:5, :], mod_ref[0, 3:4, :]).astype(BF16)

    def proj(lo, hi):
        return jnp.dot(hb_ref[...], wm_ref[:, lo:hi], preferred_element_type=F32)

    reps = ATT_WIDTH // LANES
    cos = jnp.concatenate([cos_ref[0]] * reps, axis=1)
    sin = jnp.concatenate([sin_ref[0]] * reps, axis=1)
    lane = lax.broadcasted_iota(jnp.int32, cos.shape, 1)
    first_half = (lane & (HEAD_DIM // 2)) == 0

    def norm_rope(t, g_ref):
        ss = _group_sum(t * t, bd_ref[...])
        y = (t * lax.rsqrt(ss * (1.0 / HEAD_DIM) + EPS)) * g_ref[...]
        partner = jnp.where(first_half, pltpu.roll(y, ATT_WIDTH - HEAD_DIM // 2, 1), pltpu.roll(y, HEAD_DIM // 2, 1))
        return y * cos + partner * sin

    w = ATT_WIDTH
    q_ref[...] = norm_rope(proj(0, w), qg_ref).astype(BF16)
    k_ref[...] = norm_rope(proj(w, 2 * w), kg_ref).astype(BF16)
    v_ref[...] = proj(2 * w, 3 * w).astype(BF16)
    o = 3 * w
    ug = proj(o, o + 2 * CONV_WIDTH)
    u_ref[...] = ug[:, :CONV_WIDTH] * jax.nn.sigmoid(ug[:, CONV_WIDTH:])
    o += 2 * CONV_WIDTH
    gq_ref[...] = proj(o, o + GLA_WIDTH)
    gk_ref[...] = proj(o + GLA_WIDTH, o + 2 * GLA_WIDTH)
    gv_ref[...] = proj(o + 2 * GLA_WIDTH, o + 3 * GLA_WIDTH)
    gr_ref[...] = proj(o + 3 * GLA_WIDTH, o + 4 * GLA_WIDTH)
    cg = jnp.dot(hb_ref[...], wg_ref[...], preferred_element_type=F32).astype(BF16)
    z = jnp.dot(cg, gw_ref[...], preferred_element_type=F32) + gb_ref[...]
    gd_ref[...] = _log_sigmoid(z) * (1.0 / GLA_TAU)


def _proj(x, mod, gain, w_in, gate_w, gate_b, q_gain, k_gain, cos, sin, *, rows_per_batch):
    n, d = x.shape
    main = 3 * ATT_WIDTH + 2 * CONV_WIDTH + 4 * GLA_WIDTH
    wm = w_in[:, :main].astype(BF16)
    wg = jnp.pad(w_in[:, main:], ((0, 0), (0, LANES - GLA_GATE_RANK))).astype(BF16)
    gw = jnp.pad(gate_w, ((0, LANES - GLA_GATE_RANK), (0, 0))).astype(BF16)
    bd = _block_diag_ones(ATT_WIDTH, HEAD_DIM)
    tm = min(ROW_TILE, rows_per_batch)
    tpb = rows_per_batch // tm
    row = lambda i: (i, 0)
    const = lambda i: (0, 0)
    tab = pl.BlockSpec((1, tm, LANES), lambda i: (i // tpb, i % tpb, 0))
    att = jax.ShapeDtypeStruct((n, ATT_WIDTH), BF16)
    g32 = jax.ShapeDtypeStruct((n, GLA_WIDTH), F32)
    return pl.pallas_call(
        _proj_body,
        out_shape=(att, att, att, jax.ShapeDtypeStruct((n, CONV_WIDTH), F32), g32, g32, g32, g32, g32),
        grid=(n // tm,),
        in_specs=[pl.BlockSpec((tm, d), row),
                  pl.BlockSpec((1, N_MOD, d), lambda i: (i // tpb, 0, 0)),
                  pl.BlockSpec((1, d), const),
                  _resident(wm.shape), _resident(wg.shape), _resident(gw.shape),
                  pl.BlockSpec((1, GLA_WIDTH), const),
                  pl.BlockSpec((1, ATT_WIDTH), const), pl.BlockSpec((1, ATT_WIDTH), const),
                  tab, tab, _resident(bd.shape)],
        out_specs=(pl.BlockSpec((tm, ATT_WIDTH), row),) * 3 + (pl.BlockSpec((tm, CONV_WIDTH), row),)
        + (pl.BlockSpec((tm, GLA_WIDTH), row),) * 5,
        scratch_shapes=[pltpu.VMEM((tm, d), BF16)],
        compiler_params=_cparams(("parallel",), 48 * 2**20),
        name="mix_proj",
    )(x, mod, gain.reshape(1, d), wm, wg, gw, gate_b.reshape(1, GLA_WIDTH),
      jnp.tile(q_gain * LOG2E, ATT_HEADS).reshape(1, ATT_WIDTH), jnp.tile(k_gain, ATT_HEADS).reshape(1, ATT_WIDTH),
      cos, sin, bd)


def _attn_body(q_ref, k_ref, v_ref, o_ref, vt_ref, bias_ref, s0_ref, s1_ref, qq_ref, ot_ref, *, n_blk):
    blk = MOBA_BLOCK
    seq = n_blk * blk
    heads = LANES // HEAD_DIM
    lane = lax.broadcasted_iota(jnp.int32, (1, LANES), 1)
    log2_blk = blk.bit_length() - 1
    log2_hd = HEAD_DIM.bit_length() - 1

    def vt_step(j, carry):
        r0 = pl.multiple_of(j * blk, blk)
        vt_ref[j] = v_ref[0, pl.ds(r0, blk), :].astype(F32).T.astype(BF16)
        return carry

    lax.fori_loop(0, n_blk, vt_step, 0)

    kmean = jnp.concatenate(
        [jnp.mean(k_ref[0, j * blk:(j + 1) * blk, :].astype(F32), axis=0, keepdims=True) for j in range(n_blk)], axis=0)

    jrow = lax.broadcasted_iota(jnp.int32, (n_blk, seq), 0)
    tcol = lax.broadcasted_iota(jnp.int32, (n_blk, seq), 1)
    past = jrow < lax.shift_right_logical(tcol, log2_blk)
    for h in range(heads):
        in_head = lax.shift_right_logical(lane, log2_hd) == h
        kmh = jnp.where(in_head, kmean, 0.0).astype(BF16)
        g = lax.dot_general(kmh, q_ref[0], _NT, preferred_element_type=F32)
        g = jnp.where(past, g, NEG_INF)
        rank = jnp.zeros((n_blk, seq), F32)
        for jp in range(n_blk):
            gj = g[jp:jp + 1, :]
            rank = rank + jnp.where(jrow > jp, jnp.where(gj >= g, 1.0, 0.0), jnp.where(gj > g, 1.0, 0.0))
        sel = jnp.logical_and(rank < float(min(MOBA_TOPK, n_blk)), past)
        bias = jnp.where(sel, 0.0, NEG_INF)
        for i in range(n_blk):
            bi = bias[:, i * blk:(i + 1) * blk]
            bias_ref[h, i] = jnp.broadcast_to(bi[:, None, :], (n_blk, SUBLANES, blk)).reshape(n_blk * SUBLANES, blk)

    key_row = lax.broadcasted_iota(jnp.int32, (blk, blk), 0)
    qry_col = lax.broadcasted_iota(jnp.int32, (blk, blk), 1)
    causal = key_row <= qry_col
    scale = HEAD_DIM ** -0.5
    n_past = n_blk - 1
    zero_v = jnp.zeros((HEAD_DIM, blk), BF16)
    ones_r = jnp.ones((BF16_ROWS, blk), BF16)
    zero_r = jnp.zeros((BF16_ROWS, blk), BF16)

    def k_blk(j):
        return k_ref[0, pl.ds(pl.multiple_of(j * blk, blk), blk), :]

    def tile_max(s):
        return jnp.max(s.reshape(blk // SUBLANES, SUBLANES, blk), axis=0)

    def score_pass(t, s_ref):
        ia = t
        ib = n_past - t
        maxes = []
        for h in range(heads):
            in_head = lax.shift_right_logical(lane, log2_hd) == h

            def load_q(i, in_head=in_head):
                qi = q_ref[0, pl.ds(pl.multiple_of(i * blk, blk), blk), :]
                return (jnp.where(in_head, qi, jnp.zeros_like(qi)).astype(F32) * scale).astype(BF16)

            qa = load_q(ia)
            qb = load_q(ib)
            qq_ref[h, 0] = qa
            qq_ref[h, 1] = qb
            sa = jnp.where(causal, lax.dot_general(k_blk(ia), qa, _NT, preferred_element_type=F32), NEG_INF)
            sb = jnp.where(causal, lax.dot_general(k_blk(ib), qb, _NT, preferred_element_type=F32), NEG_INF)
            s_ref[h, 0] = sa
            s_ref[h, 1] = sb
            m8a = tile_max(sa)
            m8b = tile_max(sb)
            for m in range(n_past):
                is_a = m < t
                j = jnp.where(is_a, m, m - t)
                s = lax.dot_general(k_blk(j), qq_ref[h, jnp.where(is_a, 0, 1)], _NT, preferred_element_type=F32)
                b8 = bias_ref[h, jnp.where(is_a, ia, ib), pl.ds(pl.multiple_of(j * SUBLANES, SUBLANES), SUBLANES), :]
                s3 = s.reshape(blk // SUBLANES, SUBLANES, blk) + b8[None]
                s_ref[h, 2 + m] = s3.reshape(blk, blk)
                mx = jnp.max(s3, axis=0)
                m8a = jnp.where(is_a, jnp.maximum(m8a, mx), m8a)
                m8b = jnp.where(is_a, m8b, jnp.maximum(m8b, mx))
            maxes.append(jnp.max(m8a, axis=0, keepdims=True))
            maxes.append(jnp.max(m8b, axis=0, keepdims=True))
        return tuple(maxes)

    def weighted_pass(t, maxes, s_ref):
        ia = t
        ib = n_past - t
        for h in range(heads):
            v_rows = slice(h * HEAD_DIM, (h + 1) * HEAD_DIM)
            ma = maxes[2 * h]
            mb = maxes[2 * h + 1]

            def vt_blk(j, v_rows=v_rows):
                return vt_ref[j, v_rows, :]

            pa = jnp.exp2(s_ref[h, 0] - ma).astype(BF16)
            pb = jnp.exp2(s_ref[h, 1] - mb).astype(BF16)
            acc = jnp.dot(jnp.concatenate([vt_blk(ia), zero_v, ones_r, zero_r], axis=0), pa, preferred_element_type=F32)
            acc = acc + jnp.dot(jnp.concatenate([zero_v, vt_blk(ib), zero_r, ones_r], axis=0), pb,
                                preferred_element_type=F32)
            for m in range(n_past):
                is_a = m < t
                j = jnp.where(is_a, m, m - t)
                p = jnp.exp2(s_ref[h, 2 + m] - jnp.where(is_a, ma, mb)).astype(BF16)
                vt = vt_blk(j)
                lhs = jnp.concatenate([jnp.where(is_a, vt, zero_v), jnp.where(is_a, zero_v, vt),
                                       jnp.where(is_a, ones_r, zero_r), jnp.where(is_a, zero_r, ones_r)], axis=0)
                acc = acc + jnp.dot(lhs, p, preferred_element_type=F32)
            den_a = 2 * HEAD_DIM
            den_b = den_a + BF16_ROWS
            ot_ref[ia, v_rows, :] = acc[:HEAD_DIM] / acc[den_a:den_a + 1]
            ot_ref[ib, v_rows, :] = acc[HEAD_DIM:2 * HEAD_DIM] / acc[den_b:den_b + 1]

    def two_pairs(u, maxes):
        t = 2 * u
        mid = score_pass(t + 1, s1_ref)
        weighted_pass(t, maxes, s0_ref)
        nxt = score_pass(t + 2, s0_ref)
        weighted_pass(t + 1, mid, s1_ref)
        return nxt

    n_pairs = n_blk // 2
    maxes = lax.fori_loop(0, n_pairs // 2 - 1, two_pairs, score_pass(0, s0_ref))
    mid = score_pass(n_pairs - 1, s1_ref)
    weighted_pass(n_pairs - 2, maxes, s0_ref)
    weighted_pass(n_pairs - 1, mid, s1_ref)

    def out_step(i, carry):
        r0 = pl.multiple_of(i * blk, blk)
        o_ref[0, pl.ds(r0, blk), :] = ot_ref[i].T.astype(BF16)
        return carry

    lax.fori_loop(0, n_blk, out_step, 0)


def _attn(q, k, v):
    b, s, w = q.shape
    assert s % (4 * MOBA_BLOCK) == 0
    n_blk = s // MOBA_BLOCK
    heads = LANES // HEAD_DIM
    spec = pl.BlockSpec((1, s, LANES), lambda i, p: (i, 0, p))
    return pl.pallas_call(
        functools.partial(_attn_body, n_blk=n_blk),
        out_shape=jax.ShapeDtypeStruct((b, s, w), BF16),
        grid=(b, w // LANES),
        in_specs=[spec, spec, spec],
        out_specs=spec,
        scratch_shapes=[pltpu.VMEM((n_blk, LANES, MOBA_BLOCK), BF16),
                        pltpu.VMEM((heads, n_blk, n_blk * SUBLANES, MOBA_BLOCK), F32),
                        pltpu.VMEM((heads, n_blk + 1, MOBA_BLOCK, MOBA_BLOCK), F32),
                        pltpu.VMEM((heads, n_blk + 1, MOBA_BLOCK, MOBA_BLOCK), F32),
                        pltpu.VMEM((heads, 2, MOBA_BLOCK, LANES), BF16),
                        pltpu.VMEM((n_blk, LANES, MOBA_BLOCK), F32)],
        compiler_params=_cparams(("parallel", "parallel"), 48 * 2**20),
        name="moba_attn",
    )(q, k, v)


def _conv_body(cur_ref, prev_ref, w_ref, cb_ref, g_ref, b_ref, bd_ref, o_ref, win_ref, *, tile):
    halo = prev_ref[0, tile - CONV_HALO:tile, :]
    win_ref[0:CONV_HALO] = jnp.where(pl.program_id(1) > 0, halo, 0.0)
    win_ref[CONV_HALO:CONV_HALO + tile] = cur_ref[0]
    lead = CONV_HALO - (CONV_KERNEL - 1)
    inv = 1.0 / (CONV_WIDTH // CONV_GROUPS)
    for r0 in range(0, tile, CONV_SUB):
        acc = jnp.broadcast_to(cb_ref[...], (CONV_SUB, CONV_WIDTH))
        for t in range(CONV_KERNEL):
            acc = acc + w_ref[t:t + 1, :] * win_ref[r0 + lead + t:r0 + lead + t + CONV_SUB, :]
        mu = _group_sum(acc, bd_ref[...]) * inv
        xc = acc - mu
        var = _group_sum(xc * xc, bd_ref[...]) * inv
        y = (xc * lax.rsqrt(var + EPS)) * g_ref[...] + b_ref[...]
        o_ref[0, r0:r0 + CONV_SUB, :] = (y * jax.nn.sigmoid(y)).astype(BF16)


def _conv(u, w_dw, b_dw, gain, bias):
    b, s, c = u.shape
    tile = min(SEQ_TILE, s)
    w = jnp.pad(w_dw, ((0, CONV_HALO - CONV_KERNEL), (0, 0)))
    bd = _block_diag_ones(c, c // CONV_GROUPS)
    vec = pl.BlockSpec((1, c), lambda i, j: (0, 0))
    return pl.pallas_call(
        functools.partial(_conv_body, tile=tile),
        out_shape=jax.ShapeDtypeStruct((b, s, c), BF16),
        grid=(b, s // tile),
        in_specs=[pl.BlockSpec((1, tile, c), lambda i, j: (i, j, 0)),
                  pl.BlockSpec((1, tile, c), lambda i, j: (i, jnp.maximum(j - 1, 0), 0)),
                  pl.BlockSpec((CONV_HALO, c), lambda i, j: (0, 0)),
                  vec, vec, vec, pl.BlockSpec((c, c), lambda i, j: (0, 0))],
        out_specs=pl.BlockSpec((1, tile, c), lambda i, j: (i, j, 0)),
        scratch_shapes=[pltpu.VMEM((CONV_HALO + tile, c), F32)],
        compiler_params=_cparams(("parallel", "parallel"), 32 * 2**20),
        name="conv_module",
    )(u, u, w, b_dw.reshape(1, c), gain.reshape(1, c), bias.reshape(1, c), bd)


def _gla_body(q_ref, k_ref, v_ref, g_ref, r_ref, gain_ref, bd_ref, tri_ref, o_ref, st_ref, *, tile):
    @pl.when(pl.program_id(1) == 0)
    def _():
        st_ref[...] = jnp.zeros_like(st_ref)

    w = GLA_WIDTH
    ch = GLA_CHUNK
    log2_hd = HEAD_DIM.bit_length() - 1
    lane = lax.broadcasted_iota(jnp.int32, (ch, w), 1)
    row = lax.broadcasted_iota(jnp.int32, (ch, w), 0)
    head_of_lane = lax.shift_right_logical(lane, log2_hd)
    causal = (lane & (HEAD_DIM - 1)) <= row
    r_bd = lax.shift_right_logical(lax.broadcasted_iota(jnp.int32, (w, w), 0), log2_hd)
    c_bd = lax.shift_right_logical(lax.broadcasted_iota(jnp.int32, (w, w), 1), log2_hd)
    same_head = r_bd == c_bd
    scale = HEAD_DIM ** -0.5

    def stack_heads(t):
        return jnp.concatenate([jnp.where(head_of_lane == h, t, 0.0) for h in range(GLA_HEADS)], axis=0).astype(BF16)

    for c0 in range(0, tile, ch):
        rows = slice(c0, c0 + ch)
        q = q_ref[0, rows, :]
        k = k_ref[0, rows, :]
        v = v_ref[0, rows, :]
        cum = _split_dot(g_ref[0, rows, :], tri_ref[...], 3, dims=(((1,), (0,)), ((), ())))
        last = cum[ch - 1:ch, :]
        q_t = ((q * scale) * jnp.exp(cum)).astype(BF16)
        k_t = k * jnp.exp(-cum)
        k_end = (k * jnp.exp(last - cum)).astype(BF16)
        decay = jnp.exp(last)
        a = lax.dot_general(q_t, stack_heads(k_t), _NT, preferred_element_type=F32)
        a = jnp.where(causal, a, 0.0).astype(BF16)
        o_intra = jnp.dot(a, stack_heads(v), preferred_element_type=F32)
        st = st_ref[...]
        o_inter = lax.dot_general(q_t, st.astype(BF16), _NT, preferred_element_type=F32)
        kv_t = lax.dot_general(v.astype(BF16), k_end, _TN, preferred_element_type=F32)
        st_ref[...] = st * decay + jnp.where(same_head, kv_t, 0.0)
        o = o_intra + o_inter
        ss = _group_sum(o * o, bd_ref[...])
        y = (o * lax.rsqrt(ss * (1.0 / HEAD_DIM) + EPS)) * gain_ref[...]
        r = r_ref[0, rows, :]
        o_ref[0, rows, :] = (y * (r * jax.nn.sigmoid(r))).astype(BF16)


def _gla(q, k, v, g, r, out_gain):
    b, s, w = q.shape
    tile = min(SEQ_TILE, s)
    bd = _block_diag_ones(w, HEAD_DIM)
    idx = jnp.arange(GLA_CHUNK)
    tri = (idx[None, :] <= idx[:, None]).astype(BF16)
    seq = pl.BlockSpec((1, tile, w), lambda i, j: (i, j, 0))
    return pl.pallas_call(
        functools.partial(_gla_body, tile=tile),
        out_shape=jax.ShapeDtypeStruct((b, s, w), BF16),
        grid=(b, s // tile),
        in_specs=[seq, seq, seq, seq, seq,
                  pl.BlockSpec((1, w), lambda i, j: (0, 0)),
                  pl.BlockSpec((w, w), lambda i, j: (0, 0)),
                  pl.BlockSpec((GLA_CHUNK, GLA_CHUNK), lambda i, j: (0, 0))],
        out_specs=seq,
        scratch_shapes=[pltpu.VMEM((w, w), F32)],
        compiler_params=_cparams(("parallel", "arbitrary"), 32 * 2**20),
        name="gla",
    )(q, k, v, g, r, jnp.tile(out_gain, GLA_HEADS).reshape(1, w), bd, tri)


def kernel(x, c, positions, ada_w, ada_b, ffn1_norm, ffn1_w_in, ffn1_w_out, mix_norm, mix_w_in, q_norm, k_norm, conv_w, conv_b, conv_norm_g, conv_norm_b, gla_gate_w, gla_gate_b, gla_out_norm, mix_w_out, ffn2_norm, ffn2_w_in, ffn2_w_out):
    b, s, d = x.shape
    depth = ada_w.shape[0]
    n = b * s
    mod = _ada(c, ada_w, ada_b).reshape(depth, b, N_MOD, d)
    cos, sin = _rope_tables(positions)
    xf = x.reshape(n, d)
    for l in range(depth):
        xf = _ffn(xf, mod[l], ffn1_norm[l], ffn1_w_in[l], ffn1_w_out[l], mod_row=0, rows_per_batch=s)
        q, k, v, u, gq, gk, gv, gr, gd = _proj(xf, mod[l], mix_norm[l], mix_w_in[l], gla_gate_w[l], gla_gate_b[l],
                                               q_norm[l], k_norm[l], cos, sin, rows_per_batch=s)
        o_a = _attn(q.reshape(b, s, ATT_WIDTH), k.reshape(b, s, ATT_WIDTH), v.reshape(b, s, ATT_WIDTH))
        o_b = _conv(u.reshape(b, s, CONV_WIDTH), conv_w[l], conv_b[l], conv_norm_g[l], conv_norm_b[l])
        three = lambda t: t.reshape(b, s, GLA_WIDTH)
        o_c = _gla(three(gq), three(gk), three(gv), three(gd), three(gr), gla_out_norm[l])
        mix = (o_a.reshape(n, ATT_WIDTH), o_b.reshape(n, CONV_WIDTH), o_c.reshape(n, GLA_WIDTH), mix_w_out[l])
        xf = _ffn(xf, mod[l], ffn2_norm[l], ffn2_w_in[l], ffn2_w_out[l], mod_row=6, rows_per_batch=s, mix=mix)
    return xf.reshape(b, s, d)
```

```python
import functools

import jax
import jax.numpy as jnp
from jax import lax
from jax.experimental import pallas as pl
from jax.experimental.pallas import tpu as pltpu

F32 = jnp.float32
BF16 = jnp.bfloat16

EPS = 1e-6
NEG_INF = -1e30
HEAD_DIM = 64
ATT_HEADS = 8
ATT_WIDTH = ATT_HEADS * HEAD_DIM
MOBA_BLOCK = 256
MOBA_TOPK = 3
ROPE_THETA = 10000.0
CONV_WIDTH = 256
CONV_GROUPS = 4
CONV_KERNEL = 31
GLA_HEADS = 4
GLA_WIDTH = 256
GLA_GATE_RANK = 16
GLA_TAU = 16.0
GLA_CHUNK = 64
FFN_RES = 0.5
N_MOD = 9

V7X_VMEM_BYTES = 64 * 2**20
LANES = 128
SUBLANES = 8
BF16_ROWS = 16
LOG2E = 1.4426950408889634
ROW_TILE = 512
FF_CHUNK = 256
SEQ_TILE = 512
CONV_HALO = 32
CONV_SUB = 128

_NT = (((1,), (1,)), ((), ()))
_TN = (((0,), (0,)), ((), ()))


def _cparams(semantics, vmem_bytes, flags=None):
    assert vmem_bytes < V7X_VMEM_BYTES
    return pltpu.CompilerParams(dimension_semantics=semantics, vmem_limit_bytes=vmem_bytes, flags=flags)


def _resident(shape):
    zeros = (0,) * len(shape)
    return pl.BlockSpec(shape, lambda *_: zeros, pipeline_mode=pl.Buffered(1))


def _split_dot(x, w, terms):
    acc, r = None, x
    for t in range(terms):
        p = r.astype(BF16)
        d = jnp.dot(p, w, preferred_element_type=F32)
        acc = d if acc is None else acc + d
        if t + 1 < terms:
            r = r - p.astype(F32)
    return acc


def _group_sum(x, bd):
    return _split_dot(x, bd, 2)


def _block_diag_ones(n, group):
    idx = jnp.arange(n) // group
    return (idx[:, None] == idx[None, :]).astype(BF16)


def _mod_norm(x, gain, scale, shift):
    y = x * lax.rsqrt(jnp.mean(x * x, axis=-1, keepdims=True) + EPS)
    return (y * gain) * (1.0 + scale) + shift


def _ada_body(c_ref, w_ref, b_ref, o_ref):
    c = c_ref[...]
    c_act = (c * jax.nn.sigmoid(c)).astype(BF16)
    o_ref[0] = jnp.dot(c_act, w_ref[0].astype(BF16), preferred_element_type=F32) + b_ref[0]


def _ada(c, ada_w, ada_b):
    depth, d, m = ada_w.shape
    b = c.shape[0]
    tn = 1024
    return pl.pallas_call(
        _ada_body,
        out_shape=jax.ShapeDtypeStruct((depth, b, m), F32),
        grid=(depth, m // tn),
        in_specs=[pl.BlockSpec((b, d), lambda l, j: (0, 0)),
                  pl.BlockSpec((1, d, tn), lambda l, j: (l, 0, j)),
                  pl.BlockSpec((1, 1, tn), lambda l, j: (l, 0, j))],
        out_specs=pl.BlockSpec((1, b, tn), lambda l, j: (l, 0, j)),
        compiler_params=_cparams(("parallel", "parallel"), 32 * 2**20),
        name="ada_mod",
    )(c, ada_w, ada_b.reshape(depth, 1, m))


def _rope_body(pos_ref, invf_ref, sign_ref, cos_ref, sin_ref):
    ang = pos_ref[0].astype(F32) * invf_ref[...]
    cos_ref[0] = jnp.cos(ang)
    sin_ref[0] = jnp.sin(ang) * sign_ref[...]


def _rope_tables(positions):
    b, s = positions.shape
    half = HEAD_DIM // 2
    inv_freq = 1.0 / (ROPE_THETA ** (jnp.arange(0, HEAD_DIM, 2, dtype=F32) / HEAD_DIM))
    invf = jnp.tile(inv_freq, LANES // half).reshape(1, LANES)
    sign = jnp.tile(jnp.concatenate([-jnp.ones((half,), F32), jnp.ones((half,), F32)]), LANES // HEAD_DIM).reshape(1, LANES)
    ts = min(s, 1024)
    shape = jax.ShapeDtypeStruct((b, s, LANES), F32)
    return pl.pallas_call(
        _rope_body,
        out_shape=(shape, shape),
        grid=(b, s // ts),
        in_specs=[pl.BlockSpec((1, ts, 1), lambda i, j: (i, j, 0)),
                  pl.BlockSpec((1, LANES), lambda i, j: (0, 0)),
                  pl.BlockSpec((1, LANES), lambda i, j: (0, 0))],
        out_specs=(pl.BlockSpec((1, ts, LANES), lambda i, j: (i, j, 0)),
                   pl.BlockSpec((1, ts, LANES), lambda i, j: (i, j, 0))),
        compiler_params=_cparams(("parallel", "parallel"), 32 * 2**20),
        name="rope_tables",
    )(positions.reshape(b, s, 1), invf, sign)


def _ffn_body(*refs, mod_row, d_ff, with_mix):
    if with_mix:
        x_ref, mod_ref, gain_ref, wi_ref, wo_ref, oa_ref, ob_ref, oc_ref, wm_ref, o_ref, hb_ref, acc_ref = refs
        wa = ATT_WIDTH
        wb = wa + CONV_WIDTH
        mixed = jnp.dot(oa_ref[...], wm_ref[0:wa, :], preferred_element_type=F32)
        mixed = mixed + jnp.dot(ob_ref[...], wm_ref[wa:wb, :], preferred_element_type=F32)
        mixed = mixed + jnp.dot(oc_ref[...], wm_ref[wb:wb + GLA_WIDTH, :], preferred_element_type=F32)
        x = x_ref[...] + mod_ref[0, 5:6, :] * mixed
    else:
        x_ref, mod_ref, gain_ref, wi_ref, wo_ref, o_ref, hb_ref, acc_ref = refs
        x = x_ref[...]
    shift = mod_ref[0, mod_row:mod_row + 1, :]
    scale = mod_ref[0, mod_row + 1:mod_row + 2, :]
    gate = mod_ref[0, mod_row + 2:mod_row + 3, :]
    hb_ref[...] = _mod_norm(x, gain_ref[...], scale, shift).astype(BF16)
    for c0 in range(0, d_ff, FF_CHUNK):
        a = jnp.dot(hb_ref[...], wi_ref[:, c0:c0 + FF_CHUNK], preferred_element_type=F32)
        b = jnp.dot(hb_ref[...], wi_ref[:, d_ff + c0:d_ff + c0 + FF_CHUNK], preferred_element_type=F32)
        g = (a * jax.nn.sigmoid(a) * b).astype(BF16)
        part = jnp.dot(g, wo_ref[c0:c0 + FF_CHUNK, :], preferred_element_type=F32)
        if c0 == 0:
            acc_ref[...] = part
        else:
            acc_ref[...] += part
    o_ref[...] = x + (FFN_RES * gate) * acc_ref[...]


def _ffn(x, mod, gain, w_in, w_out, *, mod_row, rows_per_batch, mix=None):
    n, d = x.shape
    d_ff = w_out.shape[0]
    assert d_ff % FF_CHUNK == 0
    tm = min(ROW_TILE, rows_per_batch)
    tiles_per_batch = rows_per_batch // tm
    row = lambda i: (i, 0)
    wi = w_in.astype(BF16)
    wo = w_out.astype(BF16)
    args = [x, mod, gain.reshape(1, d), wi, wo]
    in_specs = [pl.BlockSpec((tm, d), row),
                pl.BlockSpec((1, N_MOD, d), lambda i: (i // tiles_per_batch, 0, 0)),
                pl.BlockSpec((1, d), lambda i: (0, 0)),
                _resident(wi.shape),
                _resident(wo.shape)]
    if mix is not None:
        o_a, o_b, o_c, w_mix = mix
        wm = w_mix.astype(BF16)
        args += [o_a, o_b, o_c, wm]
        in_specs += [pl.BlockSpec((tm, ATT_WIDTH), row), pl.BlockSpec((tm, CONV_WIDTH), row),
                     pl.BlockSpec((tm, GLA_WIDTH), row), _resident(wm.shape)]
    body = functools.partial(_ffn_body, mod_row=mod_row, d_ff=d_ff, with_mix=mix is not None)
    return pl.pallas_call(
        body,
        out_shape=jax.ShapeDtypeStruct((n, d), F32),
        grid=(n // tm,),
        in_specs=in_specs,
        out_specs=pl.BlockSpec((tm, d), row),
        scratch_shapes=[pltpu.VMEM((tm, d), BF16), pltpu.VMEM((tm, d), F32)],
        compiler_params=_cparams(("parallel",), 48 * 2**20),
        name="ffn_mix" if mix is not None else "ffn",
    )(*args)


def _log_sigmoid(z):
    return jnp.minimum(z, 0.0) - jnp.log1p(jnp.exp(-jnp.abs(z)))


def _proj_body(x_ref, mod_ref, gain_ref, wm_ref, wg_ref, gw_ref, gb_ref, qg_ref, kg_ref, cos_ref, sin_ref, bd_ref,
               q_ref, k_ref, v_ref, u_ref, gq_ref, gk_ref, gv_ref, gr_ref, gd_ref, hb_ref):
    x = x_ref[...]
    hb_ref[...] = _mod_norm(x, gain_ref[...], mod_ref[0, 4:5, :], mod_ref[0, 3:4, :]).astype(BF16)

    def proj(lo, hi):
        return jnp.dot(hb_ref[...], wm_ref[:, lo:hi], preferred_element_type=F32)

    reps = ATT_WIDTH // LANES
    cos = jnp.concatenate([cos_ref[0]] * reps, axis=1)
    sin = jnp.concatenate([sin_ref[0]] * reps, axis=1)
    lane = lax.broadcasted_iota(jnp.int32, cos.shape, 1)
    first_half = (lane & (HEAD_DIM // 2)) == 0

    def norm_rope(t, g_ref):
        sq = t * t
        half = ATT_WIDTH // 2
        ss = jnp.concatenate([_group_sum(sq[:, :half], bd_ref[...]), _group_sum(sq[:, half:], bd_ref[...])], axis=1)
        y = (t * lax.rsqrt(ss * (1.0 / HEAD_DIM) + EPS)) * g_ref[...]
        partner = jnp.where(first_half, pltpu.roll(y, ATT_WIDTH - HEAD_DIM // 2, 1), pltpu.roll(y, HEAD_DIM // 2, 1))
        return y * cos + partner * sin

    w = ATT_WIDTH
    q_ref[...] = norm_rope(proj(0, w), qg_ref).astype(BF16)
    k_ref[...] = norm_rope(proj(w, 2 * w), kg_ref).astype(BF16)
    v_ref[...] = proj(2 * w, 3 * w).astype(BF16)
    o = 3 * w
    ug = proj(o, o + 2 * CONV_WIDTH)
    u_ref[...] = ug[:, :CONV_WIDTH] * jax.nn.sigmoid(ug[:, CONV_WIDTH:])
    o += 2 * CONV_WIDTH
    gq_ref[...] = proj(o, o + GLA_WIDTH)
    gk_ref[...] = proj(o + GLA_WIDTH, o + 2 * GLA_WIDTH)
    gv_ref[...] = proj(o + 2 * GLA_WIDTH, o + 3 * GLA_WIDTH)
    gr_ref[...] = proj(o + 3 * GLA_WIDTH, o + 4 * GLA_WIDTH)
    cg = jnp.dot(hb_ref[...], wg_ref[...], preferred_element_type=F32).astype(BF16)
    z = jnp.dot(cg, gw_ref[...], preferred_element_type=F32) + gb_ref[...]
    gd_ref[...] = _log_sigmoid(z) * (1.0 / GLA_TAU)


def _proj(x, mod, gain, w_in, gate_w, gate_b, q_gain, k_gain, cos, sin, *, rows_per_batch):
    n, d = x.shape
    main = 3 * ATT_WIDTH + 2 * CONV_WIDTH + 4 * GLA_WIDTH
    wm = w_in[:, :main].astype(BF16)
    wg = jnp.pad(w_in[:, main:], ((0, 0), (0, LANES - GLA_GATE_RANK))).astype(BF16)
    gw = jnp.pad(gate_w, ((0, LANES - GLA_GATE_RANK), (0, 0))).astype(BF16)
    bd = _block_diag_ones(ATT_WIDTH // 2, HEAD_DIM)
    tm = min(ROW_TILE, rows_per_batch)
    tpb = rows_per_batch // tm
    row = lambda i: (i, 0)
    const = lambda i: (0, 0)
    tab = pl.BlockSpec((1, tm, LANES), lambda i: (i // tpb, i % tpb, 0))
    att = jax.ShapeDtypeStruct((n, ATT_WIDTH), BF16)
    g32 = jax.ShapeDtypeStruct((n, GLA_WIDTH), F32)
    return pl.pallas_call(
        _proj_body,
        out_shape=(att, att, att, jax.ShapeDtypeStruct((n, CONV_WIDTH), F32), g32, g32, g32, g32, g32),
        grid=(n // tm,),
        in_specs=[pl.BlockSpec((tm, d), row),
                  pl.BlockSpec((1, N_MOD, d), lambda i: (i // tpb, 0, 0)),
                  pl.BlockSpec((1, d), const),
                  _resident(wm.shape), _resident(wg.shape), _resident(gw.shape),
                  pl.BlockSpec((1, GLA_WIDTH), const),
                  pl.BlockSpec((1, ATT_WIDTH), const), pl.BlockSpec((1, ATT_WIDTH), const),
                  tab, tab, _resident(bd.shape)],
        out_specs=(pl.BlockSpec((tm, ATT_WIDTH), row),) * 3 + (pl.BlockSpec((tm, CONV_WIDTH), row),)
        + (pl.BlockSpec((tm, GLA_WIDTH), row),) * 5,
        scratch_shapes=[pltpu.VMEM((tm, d), BF16)],
        compiler_params=_cparams(("parallel",), 48 * 2**20),
        name="mix_proj",
    )(x, mod, gain.reshape(1, d), wm, wg, gw, gate_b.reshape(1, GLA_WIDTH),
      jnp.tile(q_gain * LOG2E, ATT_HEADS).reshape(1, ATT_WIDTH), jnp.tile(k_gain, ATT_HEADS).reshape(1, ATT_WIDTH),
      cos, sin, bd)


def _attn_body(q_ref, k_ref, v_ref, o_ref, vt_ref, bias_ref, s0_ref, s1_ref, qq_ref, ot_ref, *, n_blk):
    blk = MOBA_BLOCK
    seq = n_blk * blk
    heads = LANES // HEAD_DIM
    lane = lax.broadcasted_iota(jnp.int32, (1, LANES), 1)
    log2_blk = blk.bit_length() - 1
    log2_hd = HEAD_DIM.bit_length() - 1

    for j in range(n_blk):
        vt_ref[j] = v_ref[0, j * blk:(j + 1) * blk, :].astype(F32).T.astype(BF16)

    kmean = jnp.concatenate(
        [jnp.mean(k_ref[0, j * blk:(j + 1) * blk, :].astype(F32), axis=0, keepdims=True) for j in range(n_blk)], axis=0)

    jrow = lax.broadcasted_iota(jnp.int32, (n_blk, seq), 0)
    tcol = lax.broadcasted_iota(jnp.int32, (n_blk, seq), 1)
    past = jrow < lax.shift_right_logical(tcol, log2_blk)
    for h in range(heads):
        in_head = lax.shift_right_logical(lane, log2_hd) == h
        kmh = jnp.where(in_head, kmean, 0.0).astype(BF16)
        g = lax.dot_general(kmh, q_ref[0], _NT, preferred_element_type=F32)
        g = jnp.where(past, g, NEG_INF)
        rank = jnp.zeros((n_blk, seq), F32)
        for jp in range(n_blk):
            gj = g[jp:jp + 1, :]
            rank = rank + jnp.where(jrow > jp, jnp.where(gj >= g, 1.0, 0.0), jnp.where(gj > g, 1.0, 0.0))
        sel = jnp.logical_and(rank < float(min(MOBA_TOPK, n_blk)), past)
        bias = jnp.where(sel, 0.0, NEG_INF)
        for i in range(n_blk):
            bi = bias[:, i * blk:(i + 1) * blk]
            bias_ref[h, i] = jnp.broadcast_to(bi[:, None, :], (n_blk, SUBLANES, blk)).reshape(n_blk * SUBLANES, blk)

    key_row = lax.broadcasted_iota(jnp.int32, (blk, blk), 0)
    qry_col = lax.broadcasted_iota(jnp.int32, (blk, blk), 1)
    causal = key_row <= qry_col
    scale = HEAD_DIM ** -0.5
    n_past = n_blk - 1
    zero_v = jnp.zeros((HEAD_DIM, blk), BF16)
    sum_row = lax.broadcasted_iota(jnp.int32, (BF16_ROWS, blk), 0)
    sum_a = jnp.where(sum_row == 0, 1.0, 0.0).astype(BF16)
    sum_b = jnp.where(sum_row == 1, 1.0, 0.0).astype(BF16)

    def k_blk(j):
        return k_ref[0, pl.ds(pl.multiple_of(j * blk, blk), blk), :]

    def tile_max(s):
        return jnp.max(s.reshape(blk // SUBLANES, SUBLANES, blk), axis=0)

    def score_pass(t, s_ref):
        ia = t
        ib = n_past - t
        maxes = []
        for h in range(heads):
            in_head = lax.shift_right_logical(lane, log2_hd) == h

            def load_q(i, in_head=in_head):
                qi = q_ref[0, pl.ds(pl.multiple_of(i * blk, blk), blk), :]
                return (jnp.where(in_head, qi, jnp.zeros_like(qi)).astype(F32) * scale).astype(BF16)

            qa = load_q(ia)
            qb = load_q(ib)
            qq_ref[h, 0] = qa
            qq_ref[h, 1] = qb
            sa = jnp.where(causal, lax.dot_general(k_blk(ia), qa, _NT, preferred_element_type=F32), NEG_INF)
            sb = jnp.where(causal, lax.dot_general(k_blk(ib), qb, _NT, preferred_element_type=F32), NEG_INF)
            s_ref[h, 0] = sa
            s_ref[h, 1] = sb
            m8a = tile_max(sa)
            m8b = tile_max(sb)
            for m in range(n_past):
                is_a = m < t
                j = jnp.where(is_a, m, m - t)
                s = lax.dot_general(k_blk(j), qq_ref[h, jnp.where(is_a, 0, 1)], _NT, preferred_element_type=F32)
                b8 = bias_ref[h, jnp.where(is_a, ia, ib), pl.ds(pl.multiple_of(j * SUBLANES, SUBLANES), SUBLANES), :]
                s3 = s.reshape(blk // SUBLANES, SUBLANES, blk) + b8[None]
                s_ref[h, 2 + m] = s3.reshape(blk, blk)
                mx = jnp.max(s3, axis=0)
                m8a = jnp.where(is_a, jnp.maximum(m8a, mx), m8a)
                m8b = jnp.where(is_a, m8b, jnp.maximum(m8b, mx))
            maxes.append(jnp.max(m8a, axis=0, keepdims=True))
            maxes.append(jnp.max(m8b, axis=0, keepdims=True))
        return tuple(maxes)

    def weighted_pass(t, maxes, s_ref):
        ia = t
        ib = n_past - t
        for h in range(heads):
            v_rows = slice(h * HEAD_DIM, (h + 1) * HEAD_DIM)
            ma = maxes[2 * h]
            mb = maxes[2 * h + 1]

            def vt_blk(j, v_rows=v_rows):
                return vt_ref[j, v_rows, :]

            pa = jnp.exp2(s_ref[h, 0] - ma).astype(BF16)
            pb = jnp.exp2(s_ref[h, 1] - mb).astype(BF16)
            acc = jnp.dot(jnp.concatenate([vt_blk(ia), zero_v, sum_a], axis=0), pa, preferred_element_type=F32)
            acc = acc + jnp.dot(jnp.concatenate([zero_v, vt_blk(ib), sum_b], axis=0), pb, preferred_element_type=F32)
            for m in range(n_past):
                is_a = m < t
                j = jnp.where(is_a, m, m - t)
                p = jnp.exp2(s_ref[h, 2 + m] - jnp.where(is_a, ma, mb)).astype(BF16)
                vt = vt_blk(j)
                lhs = jnp.concatenate([jnp.where(is_a, vt, zero_v), jnp.where(is_a, zero_v, vt),
                                       jnp.where(is_a, sum_a, sum_b)], axis=0)
                acc = acc + jnp.dot(lhs, p, preferred_element_type=F32)
            den = 2 * HEAD_DIM
            ot_ref[ia, v_rows, :] = acc[:HEAD_DIM] / acc[den:den + 1]
            ot_ref[ib, v_rows, :] = acc[HEAD_DIM:den] / acc[den + 1:den + 2]

    def two_pairs(u, maxes):
        t = 2 * u
        mid = score_pass(t + 1, s1_ref)
        weighted_pass(t, maxes, s0_ref)
        nxt = score_pass(t + 2, s0_ref)
        weighted_pass(t + 1, mid, s1_ref)
        return nxt

    n_pairs = n_blk // 2
    maxes = lax.fori_loop(0, n_pairs // 2 - 1, two_pairs, score_pass(0, s0_ref))
    mid = score_pass(n_pairs - 1, s1_ref)
    weighted_pass(n_pairs - 2, maxes, s0_ref)
    weighted_pass(n_pairs - 1, mid, s1_ref)

    for i in range(n_blk):
        o_ref[0, i * blk:(i + 1) * blk, :] = ot_ref[i].T.astype(BF16)


def _attn(q, k, v):
    b, s, w = q.shape
    assert s % (4 * MOBA_BLOCK) == 0
    n_blk = s // MOBA_BLOCK
    heads = LANES // HEAD_DIM
    spec = pl.BlockSpec((1, s, LANES), lambda i, p: (i, 0, p))
    return pl.pallas_call(
        functools.partial(_attn_body, n_blk=n_blk),
        out_shape=jax.ShapeDtypeStruct((b, s, w), BF16),
        grid=(b, w // LANES),
        in_specs=[spec, spec, spec],
        out_specs=spec,
        scratch_shapes=[pltpu.VMEM((n_blk, LANES, MOBA_BLOCK), BF16),
                        pltpu.VMEM((heads, n_blk, n_blk * SUBLANES, MOBA_BLOCK), F32),
                        pltpu.VMEM((heads, n_blk + 1, MOBA_BLOCK, MOBA_BLOCK), F32),
                        pltpu.VMEM((heads, n_blk + 1, MOBA_BLOCK, MOBA_BLOCK), F32),
                        pltpu.VMEM((heads, 2, MOBA_BLOCK, LANES), BF16),
                        pltpu.VMEM((n_blk, LANES, MOBA_BLOCK), F32)],
        compiler_params=_cparams(("parallel", "parallel"), 48 * 2**20),
        name="moba_attn",
    )(q, k, v)


def _conv_body(cur_ref, prev_ref, w_ref, cb_ref, g_ref, b_ref, bd_ref, o_ref, win_ref, sh_ref, *, tile):
    halo = prev_ref[0, tile - CONV_HALO:tile, :]
    win_ref[0:CONV_HALO] = jnp.where(pl.program_id(1) > 0, halo, 0.0)
    win_ref[CONV_HALO:CONV_HALO + tile] = cur_ref[0]
    n_sh = CONV_HALO + tile - SUBLANES
    for r in range(1, SUBLANES):
        sh_ref[r - 1] = win_ref[r:r + n_sh, :]
    lead = CONV_HALO - (CONV_KERNEL - 1)
    inv = 1.0 / (CONV_WIDTH // CONV_GROUPS)
    for r0 in range(0, tile, CONV_SUB):
        acc = jnp.broadcast_to(cb_ref[...], (CONV_SUB, CONV_WIDTH))
        for t in range(CONV_KERNEL):
            base, phase = divmod(lead + t, SUBLANES)
            lo = r0 + base * SUBLANES
            x = win_ref[lo:lo + CONV_SUB, :] if phase == 0 else sh_ref[phase - 1, lo:lo + CONV_SUB, :]
            acc = acc + w_ref[t:t + 1, :] * x
        mu = _group_sum(acc, bd_ref[...]) * inv
        xc = acc - mu
        var = _group_sum(xc * xc, bd_ref[...]) * inv
        y = (xc * lax.rsqrt(var + EPS)) * g_ref[...] + b_ref[...]
        o_ref[0, r0:r0 + CONV_SUB, :] = (y * jax.nn.sigmoid(y)).astype(BF16)


def _conv(u, w_dw, b_dw, gain, bias):
    b, s, c = u.shape
    tile = min(SEQ_TILE, s)
    w = jnp.pad(w_dw, ((0, CONV_HALO - CONV_KERNEL), (0, 0)))
    bd = _block_diag_ones(c, c // CONV_GROUPS)
    vec = pl.BlockSpec((1, c), lambda i, j: (0, 0))
    return pl.pallas_call(
        functools.partial(_conv_body, tile=tile),
        out_shape=jax.ShapeDtypeStruct((b, s, c), BF16),
        grid=(b, s // tile),
        in_specs=[pl.BlockSpec((1, tile, c), lambda i, j: (i, j, 0)),
                  pl.BlockSpec((1, tile, c), lambda i, j: (i, jnp.maximum(j - 1, 0), 0)),
                  pl.BlockSpec((CONV_HALO, c), lambda i, j: (0, 0)),
                  vec, vec, vec, pl.BlockSpec((c, c), lambda i, j: (0, 0))],
        out_specs=pl.BlockSpec((1, tile, c), lambda i, j: (i, j, 0)),
        scratch_shapes=[pltpu.VMEM((CONV_HALO + tile, c), F32),
                        pltpu.VMEM((SUBLANES - 1, CONV_HALO + tile - SUBLANES, c), F32)],
        compiler_params=_cparams(("parallel", "parallel"), 32 * 2**20),
        name="conv_module",
    )(u, u, w, b_dw.reshape(1, c), gain.reshape(1, c), bias.reshape(1, c), bd)


def _chunk_cumsum(x):
    n = x.shape[0]
    row = lax.broadcasted_iota(jnp.int32, x.shape, 0)
    step = 1
    while step < n:
        x = x + jnp.where(row >= step, pltpu.roll(x, step, 0), 0.0)
        step *= 2
    return x


def _gla_body(q_ref, k_ref, v_ref, g_ref, r_ref, gain_ref, bd_ref, o_ref, st_ref, raw_ref, *, tile):
    @pl.when(pl.program_id(1) == 0)
    def _():
        st_ref[...] = jnp.zeros_like(st_ref)

    w = GLA_WIDTH
    ch = GLA_CHUNK
    log2_hd = HEAD_DIM.bit_length() - 1
    lane = lax.broadcasted_iota(jnp.int32, (ch, w), 1)
    row = lax.broadcasted_iota(jnp.int32, (ch, w), 0)
    head_of_lane = lax.shift_right_logical(lane, log2_hd)
    causal = (lane & (HEAD_DIM - 1)) <= row
    r_bd = lax.shift_right_logical(lax.broadcasted_iota(jnp.int32, (w, w), 0), log2_hd)
    c_bd = lax.shift_right_logical(lax.broadcasted_iota(jnp.int32, (w, w), 1), log2_hd)
    same_head = r_bd == c_bd
    scale = HEAD_DIM ** -0.5

    def stack_heads(t):
        return jnp.concatenate([jnp.where(head_of_lane == h, t, 0.0) for h in range(GLA_HEADS)], axis=0).astype(BF16)

    for c0 in range(0, tile, ch):
        rows = slice(c0, c0 + ch)
        q = q_ref[0, rows, :]
        k = k_ref[0, rows, :]
        v = v_ref[0, rows, :]
        cum = _chunk_cumsum(g_ref[0, rows, :])
        last = cum[ch - 1:ch, :]
        q_t = ((q * scale) * jnp.exp(cum)).astype(BF16)
        k_t = k * jnp.exp(-cum)
        k_end = (k * jnp.exp(last - cum)).astype(BF16)
        decay = jnp.exp(last)
        a = lax.dot_general(q_t, stack_heads(k_t), _NT, preferred_element_type=F32)
        a = jnp.where(causal, a, 0.0).astype(BF16)
        o_intra = jnp.dot(a, stack_heads(v), preferred_element_type=F32)
        st = st_ref[...]
        o_inter = lax.dot_general(q_t, st.astype(BF16), _NT, preferred_element_type=F32)
        kv_t = lax.dot_general(v.astype(BF16), k_end, _TN, preferred_element_type=F32)
        st_ref[...] = st * decay + jnp.where(same_head, kv_t, 0.0)
        raw_ref[rows, :] = o_intra + o_inter
    o = raw_ref[...]
    ss = _group_sum(o * o, bd_ref[...])
    y = (o * lax.rsqrt(ss * (1.0 / HEAD_DIM) + EPS)) * gain_ref[...]
    r = r_ref[0]
    o_ref[0] = (y * (r * jax.nn.sigmoid(r))).astype(BF16)


def _gla(q, k, v, g, r, out_gain):
    b, s, w = q.shape
    tile = min(SEQ_TILE, s)
    bd = _block_diag_ones(w, HEAD_DIM)
    seq = pl.BlockSpec((1, tile, w), lambda i, j: (i, j, 0))
    return pl.pallas_call(
        functools.partial(_gla_body, tile=tile),
        out_shape=jax.ShapeDtypeStruct((b, s, w), BF16),
        grid=(b, s // tile),
        in_specs=[seq, seq, seq, seq, seq,
                  pl.BlockSpec((1, w), lambda i, j: (0, 0)),
                  pl.BlockSpec((w, w), lambda i, j: (0, 0))],
        out_specs=seq,
        scratch_shapes=[pltpu.VMEM((w, w), F32), pltpu.VMEM((tile, w), F32)],
        compiler_params=_cparams(("parallel", "arbitrary"), 32 * 2**20),
        name="gla",
    )(q, k, v, g, r, jnp.tile(out_gain, GLA_HEADS).reshape(1, w), bd)


def kernel(x, c, positions, ada_w, ada_b, ffn1_norm, ffn1_w_in, ffn1_w_out, mix_norm, mix_w_in, q_norm, k_norm, conv_w, conv_b, conv_norm_g, conv_norm_b, gla_gate_w, gla_gate_b, gla_out_norm, mix_w_out, ffn2_norm, ffn2_w_in, ffn2_w_out):
    b, s, d = x.shape
    depth = ada_w.shape[0]
    n = b * s
    mod = _ada(c, ada_w, ada_b).reshape(depth, b, N_MOD, d)
    cos, sin = _rope_tables(positions)
    xf = x.reshape(n, d)
    for l in range(depth):
        xf = _ffn(xf, mod[l], ffn1_norm[l], ffn1_w_in[l], ffn1_w_out[l], mod_row=0, rows_per_batch=s)
        q, k, v, u, gq, gk, gv, gr, gd = _proj(xf, mod[l], mix_norm[l], mix_w_in[l], gla_gate_w[l], gla_gate_b[l],
                                               q_norm[l], k_norm[l], cos, sin, rows_per_batch=s)
        o_a = _attn(q.reshape(b, s, ATT_WIDTH), k.reshape(b, s, ATT_WIDTH), v.reshape(b, s, ATT_WIDTH))
        o_b = _conv(u.reshape(b, s, CONV_WIDTH), conv_w[l], conv_b[l], conv_norm_g[l], conv_norm_b[l])
        three = lambda t: t.reshape(b, s, GLA_WIDTH)
        o_c = _gla(three(gq), three(gk), three(gv), three(gd), three(gr), gla_out_norm[l])
        mix = (o_a.reshape(n, ATT_WIDTH), o_b.reshape(n, CONV_WIDTH), o_c.reshape(n, GLA_WIDTH), mix_w_out[l])
        xf = _ffn(xf, mod[l], ffn2_norm[l], ffn2_w_in[l], ffn2_w_out[l], mod_row=6, rows_per_batch=s, mix=mix)
    return xf.reshape(b, s, d)
```

```python
import functools

import jax
import jax.numpy as jnp
from jax import lax
from jax.experimental import pallas as pl
from jax.experimental.pallas import tpu as pltpu

F32 = jnp.float32
BF16 = jnp.bfloat16

EPS = 1e-6
NEG_INF = -1e30
HEAD_DIM = 64
ATT_HEADS = 8
ATT_WIDTH = ATT_HEADS * HEAD_DIM
MOBA_BLOCK = 256
MOBA_TOPK = 3
ROPE_THETA = 10000.0
CONV_WIDTH = 256
CONV_GROUPS = 4
CONV_KERNEL = 31
GLA_HEADS = 4
GLA_WIDTH = 256
GLA_GATE_RANK = 16
GLA_TAU = 16.0
GLA_CHUNK = 64
FFN_RES = 0.5
N_MOD = 9

V7X_VMEM_BYTES = 64 * 2**20
LANES = 128
SUBLANES = 8
BF16_ROWS = 16
LOG2E = 1.4426950408889634
ROW_TILE = 512
FF_CHUNK = 256
SEQ_TILE = 512
CONV_HALO = 32
CONV_SUB = 128

_NT = (((1,), (1,)), ((), ()))
_TN = (((0,), (0,)), ((), ()))


def _cparams(semantics, vmem_bytes, flags=None):
    assert vmem_bytes < V7X_VMEM_BYTES
    return pltpu.CompilerParams(dimension_semantics=semantics, vmem_limit_bytes=vmem_bytes, flags=flags)


def _resident(shape):
    zeros = (0,) * len(shape)
    return pl.BlockSpec(shape, lambda *_: zeros, pipeline_mode=pl.Buffered(1))


def _split_dot(x, w, terms):
    acc, r = None, x
    for t in range(terms):
        p = r.astype(BF16)
        d = jnp.dot(p, w, preferred_element_type=F32)
        acc = d if acc is None else acc + d
        if t + 1 < terms:
            r = r - p.astype(F32)
    return acc


def _group_sum(x, bd):
    return _split_dot(x, bd, 2)


def _block_diag_ones(n, group):
    idx = jnp.arange(n) // group
    return (idx[:, None] == idx[None, :]).astype(BF16)


def _mod_norm(x, gain, scale, shift):
    y = x * lax.rsqrt(jnp.mean(x * x, axis=-1, keepdims=True) + EPS)
    return (y * gain) * (1.0 + scale) + shift


def _cast_body(w_ref, o_ref):
    o_ref[...] = w_ref[0].astype(BF16)


def _layer_bf16(w, layer):
    _, rows, cols = w.shape
    steps = 4
    assert rows % (steps * BF16_ROWS) == 0
    tr = rows // steps
    return pl.pallas_call(
        _cast_body,
        out_shape=jax.ShapeDtypeStruct((rows, cols), BF16),
        grid=(steps,),
        in_specs=[pl.BlockSpec((1, tr, cols), lambda i: (layer, i, 0))],
        out_specs=pl.BlockSpec((tr, cols), lambda i: (i, 0)),
        compiler_params=_cparams(("parallel",), 40 * 2**20),
        name="cast_bf16",
    )(w)


def _ada_body(c_ref, w_ref, b_ref, o_ref):
    c = c_ref[...]
    c_act = (c * jax.nn.sigmoid(c)).astype(BF16)
    o_ref[0] = jnp.dot(c_act, w_ref[0].astype(BF16), preferred_element_type=F32) + b_ref[0]


def _ada(c, ada_w, ada_b):
    depth, d, m = ada_w.shape
    b = c.shape[0]
    tn = 2304
    assert m % tn == 0
    return pl.pallas_call(
        _ada_body,
        out_shape=jax.ShapeDtypeStruct((depth, b, m), F32),
        grid=(depth, m // tn),
        in_specs=[pl.BlockSpec((b, d), lambda l, j: (0, 0)),
                  pl.BlockSpec((1, d, tn), lambda l, j: (l, 0, j)),
                  pl.BlockSpec((1, 1, tn), lambda l, j: (l, 0, j))],
        out_specs=pl.BlockSpec((1, b, tn), lambda l, j: (l, 0, j)),
        compiler_params=_cparams(("parallel", "parallel"), 32 * 2**20),
        name="ada_mod",
    )(c, ada_w, ada_b.reshape(depth, 1, m))


def _rope_body(pos_ref, invf_ref, sign_ref, cos_ref, sin_ref):
    ang = pos_ref[0].astype(F32) * invf_ref[...]
    cos_ref[0] = jnp.cos(ang)
    sin_ref[0] = jnp.sin(ang) * sign_ref[...]


def _rope_tables(positions):
    b, s = positions.shape
    half = HEAD_DIM // 2
    inv_freq = 1.0 / (ROPE_THETA ** (jnp.arange(0, HEAD_DIM, 2, dtype=F32) / HEAD_DIM))
    invf = jnp.tile(inv_freq, LANES // half).reshape(1, LANES)
    sign = jnp.tile(jnp.concatenate([-jnp.ones((half,), F32), jnp.ones((half,), F32)]), LANES // HEAD_DIM).reshape(1, LANES)
    ts = min(s, 1024)
    shape = jax.ShapeDtypeStruct((b, s, LANES), F32)
    return pl.pallas_call(
        _rope_body,
        out_shape=(shape, shape),
        grid=(b, s // ts),
        in_specs=[pl.BlockSpec((1, ts, 1), lambda i, j: (i, j, 0)),
                  pl.BlockSpec((1, LANES), lambda i, j: (0, 0)),
                  pl.BlockSpec((1, LANES), lambda i, j: (0, 0))],
        out_specs=(pl.BlockSpec((1, ts, LANES), lambda i, j: (i, j, 0)),
                   pl.BlockSpec((1, ts, LANES), lambda i, j: (i, j, 0))),
        compiler_params=_cparams(("parallel", "parallel"), 32 * 2**20),
        name="rope_tables",
    )(positions.reshape(b, s, 1), invf, sign)


def _ffn_body(*refs, mod_row, d_ff, with_mix):
    if with_mix:
        x_ref, mod_ref, gain_ref, wi_ref, wo_ref, oa_ref, ob_ref, oc_ref, wm_ref, o_ref, hb_ref, acc_ref = refs
        wa = ATT_WIDTH
        wb = wa + CONV_WIDTH
        mixed = jnp.dot(oa_ref[...], wm_ref[0:wa, :], preferred_element_type=F32)
        mixed = mixed + jnp.dot(ob_ref[...], wm_ref[wa:wb, :], preferred_element_type=F32)
        mixed = mixed + jnp.dot(oc_ref[...], wm_ref[wb:wb + GLA_WIDTH, :], preferred_element_type=F32)
        x = x_ref[...] + mod_ref[0, 5:6, :] * mixed
    else:
        x_ref, mod_ref, gain_ref, wi_ref, wo_ref, o_ref, hb_ref, acc_ref = refs
        x = x_ref[...]
    shift = mod_ref[0, mod_row:mod_row + 1, :]
    scale = mod_ref[0, mod_row + 1:mod_row + 2, :]
    gate = mod_ref[0, mod_row + 2:mod_row + 3, :]
    hb_ref[...] = _mod_norm(x, gain_ref[...], scale, shift).astype(BF16)
    for c0 in range(0, d_ff, FF_CHUNK):
        a = jnp.dot(hb_ref[...], wi_ref[:, c0:c0 + FF_CHUNK], preferred_element_type=F32)
        b = jnp.dot(hb_ref[...], wi_ref[:, d_ff + c0:d_ff + c0 + FF_CHUNK], preferred_element_type=F32)
        g = (a * jax.nn.sigmoid(a) * b).astype(BF16)
        part = jnp.dot(g, wo_ref[c0:c0 + FF_CHUNK, :], preferred_element_type=F32)
        if c0 == 0:
            acc_ref[...] = part
        else:
            acc_ref[...] += part
    o_ref[...] = x + (FFN_RES * gate) * acc_ref[...]


def _ffn(x, mod, gain, wi, wo, *, mod_row, rows_per_batch, mix=None):
    n, d = x.shape
    d_ff = wo.shape[0]
    assert d_ff % FF_CHUNK == 0
    tm = min(ROW_TILE, rows_per_batch)
    tiles_per_batch = rows_per_batch // tm
    row = lambda i: (i, 0)
    args = [x, mod, gain.reshape(1, d), wi, wo]
    in_specs = [pl.BlockSpec((tm, d), row),
                pl.BlockSpec((1, N_MOD, d), lambda i: (i // tiles_per_batch, 0, 0)),
                pl.BlockSpec((1, d), lambda i: (0, 0)),
                _resident(wi.shape),
                _resident(wo.shape)]
    if mix is not None:
        o_a, o_b, o_c, wm = mix
        args += [o_a, o_b, o_c, wm]
        in_specs += [pl.BlockSpec((tm, ATT_WIDTH), row), pl.BlockSpec((tm, CONV_WIDTH), row),
                     pl.BlockSpec((tm, GLA_WIDTH), row), _resident(wm.shape)]
    body = functools.partial(_ffn_body, mod_row=mod_row, d_ff=d_ff, with_mix=mix is not None)
    return pl.pallas_call(
        body,
        out_shape=jax.ShapeDtypeStruct((n, d), F32),
        grid=(n // tm,),
        in_specs=in_specs,
        out_specs=pl.BlockSpec((tm, d), row),
        scratch_shapes=[pltpu.VMEM((tm, d), BF16), pltpu.VMEM((tm, d), F32)],
        compiler_params=_cparams(("parallel",), 48 * 2**20),
        name="ffn_mix" if mix is not None else "ffn",
    )(*args)


def _log_sigmoid(z):
    return jnp.minimum(z, 0.0) - jnp.log1p(jnp.exp(-jnp.abs(z)))


def _proj_body(x_ref, mod_ref, gain_ref, wm_ref, wg_ref, gw_ref, gb_ref, qg_ref, kg_ref, cos_ref, sin_ref, bd_ref,
               q_ref, k_ref, v_ref, u_ref, gq_ref, gk_ref, gv_ref, gr_ref, gd_ref, hb_ref):
    x = x_ref[...]
    hb_ref[...] = _mod_norm(x, gain_ref[...], mod_ref[0, 4:5, :], mod_ref[0, 3:4, :]).astype(BF16)

    def proj(lo, hi):
        return jnp.dot(hb_ref[...], wm_ref[:, lo:hi], preferred_element_type=F32)

    reps = ATT_WIDTH // LANES
    cos = jnp.concatenate([cos_ref[0]] * reps, axis=1)
    sin = jnp.concatenate([sin_ref[0]] * reps, axis=1)
    lane = lax.broadcasted_iota(jnp.int32, cos.shape, 1)
    first_half = (lane & (HEAD_DIM // 2)) == 0

    def norm_rope(t, g_ref):
        sq = t * t
        half = ATT_WIDTH // 2
        ss = jnp.concatenate([_group_sum(sq[:, :half], bd_ref[...]), _group_sum(sq[:, half:], bd_ref[...])], axis=1)
        y = (t * lax.rsqrt(ss * (1.0 / HEAD_DIM) + EPS)) * g_ref[...]
        partner = jnp.where(first_half, pltpu.roll(y, ATT_WIDTH - HEAD_DIM // 2, 1), pltpu.roll(y, HEAD_DIM // 2, 1))
        return y * cos + partner * sin

    w = ATT_WIDTH
    q_ref[...] = norm_rope(proj(0, w), qg_ref).astype(BF16)
    k_ref[...] = norm_rope(proj(w, 2 * w), kg_ref).astype(BF16)
    v_ref[...] = proj(2 * w, 3 * w).astype(BF16)
    o = 3 * w
    ug = proj(o, o + 2 * CONV_WIDTH)
    u_ref[...] = ug[:, :CONV_WIDTH] * jax.nn.sigmoid(ug[:, CONV_WIDTH:])
    o += 2 * CONV_WIDTH
    gq_ref[...] = proj(o, o + GLA_WIDTH)
    gk_ref[...] = proj(o + GLA_WIDTH, o + 2 * GLA_WIDTH)
    gv_ref[...] = proj(o + 2 * GLA_WIDTH, o + 3 * GLA_WIDTH)
    gr_ref[...] = proj(o + 3 * GLA_WIDTH, o + 4 * GLA_WIDTH)
    cg = jnp.dot(hb_ref[...], wg_ref[...], preferred_element_type=F32).astype(BF16)
    z = jnp.dot(cg, gw_ref[...], preferred_element_type=F32) + gb_ref[...]
    gd_ref[...] = _log_sigmoid(z) * (1.0 / GLA_TAU)


def _mix_in_body(w_ref, wm_ref, wg_ref):
    w = w_ref[0]
    main = wm_ref.shape[1]
    wm_ref[...] = w[:, :main].astype(BF16)
    tail = w[:, main:]
    pad = jnp.zeros((tail.shape[0], LANES - tail.shape[1]), F32)
    wg_ref[...] = jnp.concatenate([tail, pad], axis=1).astype(BF16)


def _mix_in_weights(w_in, layer):
    _, rows, cols = w_in.shape
    main = 3 * ATT_WIDTH + 2 * CONV_WIDTH + 4 * GLA_WIDTH
    assert cols == main + GLA_GATE_RANK
    steps = 4
    tr = rows // steps
    return pl.pallas_call(
        _mix_in_body,
        out_shape=(jax.ShapeDtypeStruct((rows, main), BF16), jax.ShapeDtypeStruct((rows, LANES), BF16)),
        grid=(steps,),
        in_specs=[pl.BlockSpec((1, tr, cols), lambda i: (layer, i, 0))],
        out_specs=(pl.BlockSpec((tr, main), lambda i: (i, 0)), pl.BlockSpec((tr, LANES), lambda i: (i, 0))),
        compiler_params=_cparams(("parallel",), 32 * 2**20),
        name="cast_mix_in",
    )(w_in)


def _proj(x, mod, gain, wm, wg, gate_w, gate_b, q_gain, k_gain, cos, sin, *, rows_per_batch):
    n, d = x.shape
    gw = jnp.pad(gate_w, ((0, LANES - GLA_GATE_RANK), (0, 0))).astype(BF16)
    bd = _block_diag_ones(ATT_WIDTH // 2, HEAD_DIM)
    tm = min(ROW_TILE, rows_per_batch)
    tpb = rows_per_batch // tm
    row = lambda i: (i, 0)
    const = lambda i: (0, 0)
    tab = pl.BlockSpec((1, tm, LANES), lambda i: (i // tpb, i % tpb, 0))
    att = jax.ShapeDtypeStruct((n, ATT_WIDTH), BF16)
    g32 = jax.ShapeDtypeStruct((n, GLA_WIDTH), F32)
    return pl.pallas_call(
        _proj_body,
        out_shape=(att, att, att, jax.ShapeDtypeStruct((n, CONV_WIDTH), F32), g32, g32, g32, g32, g32),
        grid=(n // tm,),
        in_specs=[pl.BlockSpec((tm, d), row),
                  pl.BlockSpec((1, N_MOD, d), lambda i: (i // tpb, 0, 0)),
                  pl.BlockSpec((1, d), const),
                  _resident(wm.shape), _resident(wg.shape), _resident(gw.shape),
                  pl.BlockSpec((1, GLA_WIDTH), const),
                  pl.BlockSpec((1, ATT_WIDTH), const), pl.BlockSpec((1, ATT_WIDTH), const),
                  tab, tab, _resident(bd.shape)],
        out_specs=(pl.BlockSpec((tm, ATT_WIDTH), row),) * 3 + (pl.BlockSpec((tm, CONV_WIDTH), row),)
        + (pl.BlockSpec((tm, GLA_WIDTH), row),) * 5,
        scratch_shapes=[pltpu.VMEM((tm, d), BF16)],
        compiler_params=_cparams(("parallel",), 48 * 2**20),
        name="mix_proj",
    )(x, mod, gain.reshape(1, d), wm, wg, gw, gate_b.reshape(1, GLA_WIDTH),
      jnp.tile(q_gain * LOG2E, ATT_HEADS).reshape(1, ATT_WIDTH), jnp.tile(k_gain, ATT_HEADS).reshape(1, ATT_WIDTH),
      cos, sin, bd)


def _attn_body(q_ref, k_ref, v_ref, o_ref, vt_ref, b0_ref, b1_ref, s0_ref, s1_ref, qq_ref, ot_ref, *, n_blk):
    blk = MOBA_BLOCK
    heads = LANES // HEAD_DIM
    lane = lax.broadcasted_iota(jnp.int32, (1, LANES), 1)
    log2_hd = HEAD_DIM.bit_length() - 1

    for j in range(n_blk):
        vt_ref[j] = v_ref[0, j * blk:(j + 1) * blk, :].astype(F32).T.astype(BF16)

    kmean = jnp.concatenate(
        [jnp.mean(k_ref[0, j * blk:(j + 1) * blk, :].astype(F32), axis=0, keepdims=True) for j in range(n_blk)], axis=0)

    jrow = lax.broadcasted_iota(jnp.int32, (n_blk, blk), 0)
    kmean_h = [jnp.where(lax.shift_right_logical(lane, log2_hd) == h, kmean, 0.0).astype(BF16) for h in range(heads)]

    def selection_bias(h, q_rows, i):
        g = lax.dot_general(kmean_h[h], q_rows, _NT, preferred_element_type=F32)
        past = jrow < i
        g = jnp.where(past, g, NEG_INF)
        rank = jnp.zeros((n_blk, blk), F32)
        for jp in range(n_blk):
            gj = g[jp:jp + 1, :]
            rank = rank + jnp.where(jrow > jp, jnp.where(gj >= g, 1.0, 0.0), jnp.where(gj > g, 1.0, 0.0))
        sel = jnp.logical_and(rank < float(min(MOBA_TOPK, n_blk)), past)
        bias = jnp.where(sel, 0.0, NEG_INF)
        return jnp.broadcast_to(bias[:, None, :], (n_blk, SUBLANES, blk)).reshape(n_blk * SUBLANES, blk)

    key_row = lax.broadcasted_iota(jnp.int32, (blk, blk), 0)
    qry_col = lax.broadcasted_iota(jnp.int32, (blk, blk), 1)
    causal = key_row <= qry_col
    scale = HEAD_DIM ** -0.5
    n_past = n_blk - 1
    zero_v = jnp.zeros((HEAD_DIM, blk), BF16)
    sum_row = lax.broadcasted_iota(jnp.int32, (BF16_ROWS, blk), 0)
    sum_a = jnp.where(sum_row == 0, 1.0, 0.0).astype(BF16)
    sum_b = jnp.where(sum_row == 1, 1.0, 0.0).astype(BF16)

    def k_blk(j):
        return k_ref[0, pl.ds(pl.multiple_of(j * blk, blk), blk), :]

    def tile_max(s):
        return jnp.max(s.reshape(blk // SUBLANES, SUBLANES, blk), axis=0)

    def score_pass(t, s_ref, b_ref):
        ia = t
        ib = n_past - t
        q_a = q_ref[0, pl.ds(pl.multiple_of(ia * blk, blk), blk), :]
        q_b = q_ref[0, pl.ds(pl.multiple_of(ib * blk, blk), blk), :]
        maxes = []
        for h in range(heads):
            in_head = lax.shift_right_logical(lane, log2_hd) == h

            def head_q(qi, in_head=in_head):
                return (jnp.where(in_head, qi, jnp.zeros_like(qi)).astype(F32) * scale).astype(BF16)

            b_ref[h, 0] = selection_bias(h, q_a, ia)
            b_ref[h, 1] = selection_bias(h, q_b, ib)
            qa = head_q(q_a)
            qb = head_q(q_b)
            qq_ref[h, 0] = qa
            qq_ref[h, 1] = qb
            sa = jnp.where(causal, lax.dot_general(k_blk(ia), qa, _NT, preferred_element_type=F32), NEG_INF)
            sb = jnp.where(causal, lax.dot_general(k_blk(ib), qb, _NT, preferred_element_type=F32), NEG_INF)
            s_ref[h, 0] = sa
            s_ref[h, 1] = sb
            m8a = tile_max(sa)
            m8b = tile_max(sb)
            for m in range(n_past):
                is_a = m < t
                j = jnp.where(is_a, m, m - t)
                s = lax.dot_general(k_blk(j), qq_ref[h, jnp.where(is_a, 0, 1)], _NT, preferred_element_type=F32)
                b8 = b_ref[h, jnp.where(is_a, 0, 1), pl.ds(pl.multiple_of(j * SUBLANES, SUBLANES), SUBLANES), :]
                s3 = s.reshape(blk // SUBLANES, SUBLANES, blk) + b8[None]
                s_ref[h, 2 + m] = s3.reshape(blk, blk)
                mx = jnp.max(s3, axis=0)
                m8a = jnp.where(is_a, jnp.maximum(m8a, mx), m8a)
                m8b = jnp.where(is_a, m8b, jnp.maximum(m8b, mx))
            maxes.append(jnp.max(m8a, axis=0, keepdims=True))
            maxes.append(jnp.max(m8b, axis=0, keepdims=True))
        return tuple(maxes)

    def weighted_pass(t, maxes, s_ref):
        ia = t
        ib = n_past - t
        for h in range(heads):
            v_rows = slice(h * HEAD_DIM, (h + 1) * HEAD_DIM)
            ma = maxes[2 * h]
            mb = maxes[2 * h + 1]

            def vt_blk(j, v_rows=v_rows):
                return vt_ref[j, v_rows, :]

            pa = jnp.exp2(s_ref[h, 0] - ma).astype(BF16)
            pb = jnp.exp2(s_ref[h, 1] - mb).astype(BF16)
            acc = jnp.dot(jnp.concatenate([vt_blk(ia), zero_v, sum_a], axis=0), pa, preferred_element_type=F32)
            acc = acc + jnp.dot(jnp.concatenate([zero_v, vt_blk(ib), sum_b], axis=0), pb, preferred_element_type=F32)
            for m in range(n_past):
                is_a = m < t
                j = jnp.where(is_a, m, m - t)
                p = jnp.exp2(s_ref[h, 2 + m] - jnp.where(is_a, ma, mb)).astype(BF16)
                vt = vt_blk(j)
                lhs = jnp.concatenate([jnp.where(is_a, vt, zero_v), jnp.where(is_a, zero_v, vt),
                                       jnp.where(is_a, sum_a, sum_b)], axis=0)
                acc = acc + jnp.dot(lhs, p, preferred_element_type=F32)
            den = 2 * HEAD_DIM
            ot_ref[ia, v_rows, :] = acc[:HEAD_DIM] / acc[den:den + 1]
            ot_ref[ib, v_rows, :] = acc[HEAD_DIM:den] / acc[den + 1:den + 2]

    def two_pairs(u, maxes):
        t = 2 * u
        mid = score_pass(t + 1, s1_ref, b1_ref)
        weighted_pass(t, maxes, s0_ref)
        nxt = score_pass(t + 2, s0_ref, b0_ref)
        weighted_pass(t + 1, mid, s1_ref)
        return nxt

    n_pairs = n_blk // 2
    maxes = lax.fori_loop(0, n_pairs // 2 - 1, two_pairs, score_pass(0, s0_ref, b0_ref))
    mid = score_pass(n_pairs - 1, s1_ref, b1_ref)
    weighted_pass(n_pairs - 2, maxes, s0_ref)
    weighted_pass(n_pairs - 1, mid, s1_ref)

    for i in range(n_blk):
        o_ref[0, i * blk:(i + 1) * blk, :] = ot_ref[i].T.astype(BF16)


def _attn(q, k, v):
    b, s, w = q.shape
    assert s % (4 * MOBA_BLOCK) == 0
    n_blk = s // MOBA_BLOCK
    heads = LANES // HEAD_DIM
    spec = pl.BlockSpec((1, s, LANES), lambda i, p: (i, 0, p))
    return pl.pallas_call(
        functools.partial(_attn_body, n_blk=n_blk),
        out_shape=jax.ShapeDtypeStruct((b, s, w), BF16),
        grid=(b, w // LANES),
        in_specs=[spec, spec, spec],
        out_specs=spec,
        scratch_shapes=[pltpu.VMEM((n_blk, LANES, MOBA_BLOCK), BF16),
                        pltpu.VMEM((heads, 2, n_blk * SUBLANES, MOBA_BLOCK), F32),
                        pltpu.VMEM((heads, 2, n_blk * SUBLANES, MOBA_BLOCK), F32),
                        pltpu.VMEM((heads, n_blk + 1, MOBA_BLOCK, MOBA_BLOCK), F32),
                        pltpu.VMEM((heads, n_blk + 1, MOBA_BLOCK, MOBA_BLOCK), F32),
                        pltpu.VMEM((heads, 2, MOBA_BLOCK, LANES), BF16),
                        pltpu.VMEM((n_blk, LANES, MOBA_BLOCK), F32)],
        compiler_params=_cparams(("parallel", "parallel"), 48 * 2**20),
        name="moba_attn",
    )(q, k, v)


def _conv_body(cur_ref, prev_ref, w_ref, cb_ref, g_ref, b_ref, bd_ref, o_ref, win_ref, sh_ref, *, tile):
    halo = prev_ref[0, tile - CONV_HALO:tile, :]
    win_ref[0:CONV_HALO] = jnp.where(pl.program_id(1) > 0, halo, 0.0)
    win_ref[CONV_HALO:CONV_HALO + tile] = cur_ref[0]
    n_sh = CONV_HALO + tile - SUBLANES
    for r in range(1, SUBLANES):
        sh_ref[r - 1] = win_ref[r:r + n_sh, :]
    lead = CONV_HALO - (CONV_KERNEL - 1)
    inv = 1.0 / (CONV_WIDTH // CONV_GROUPS)
    for r0 in range(0, tile, CONV_SUB):
        acc = jnp.broadcast_to(cb_ref[...], (CONV_SUB, CONV_WIDTH))
        for t in range(CONV_KERNEL):
            base, phase = divmod(lead + t, SUBLANES)
            lo = r0 + base * SUBLANES
            x = win_ref[lo:lo + CONV_SUB, :] if phase == 0 else sh_ref[phase - 1, lo:lo + CONV_SUB, :]
            acc = acc + w_ref[t:t + 1, :] * x
        mu = _group_sum(acc, bd_ref[...]) * inv
        xc = acc - mu
        var = _group_sum(xc * xc, bd_ref[...]) * inv
        y = (xc * lax.rsqrt(var + EPS)) * g_ref[...] + b_ref[...]
        o_ref[0, r0:r0 + CONV_SUB, :] = (y * jax.nn.sigmoid(y)).astype(BF16)


def _conv(u, w_dw, b_dw, gain, bias):
    b, s, c = u.shape
    tile = min(SEQ_TILE, s)
    w = jnp.pad(w_dw, ((0, CONV_HALO - CONV_KERNEL), (0, 0)))
    bd = _block_diag_ones(c, c // CONV_GROUPS)
    vec = pl.BlockSpec((1, c), lambda i, j: (0, 0))
    return pl.pallas_call(
        functools.partial(_conv_body, tile=tile),
        out_shape=jax.ShapeDtypeStruct((b, s, c), BF16),
        grid=(b, s // tile),
        in_specs=[pl.BlockSpec((1, tile, c), lambda i, j: (i, j, 0)),
                  pl.BlockSpec((1, tile, c), lambda i, j: (i, jnp.maximum(j - 1, 0), 0)),
                  pl.BlockSpec((CONV_HALO, c), lambda i, j: (0, 0)),
                  vec, vec, vec, pl.BlockSpec((c, c), lambda i, j: (0, 0))],
        out_specs=pl.BlockSpec((1, tile, c), lambda i, j: (i, j, 0)),
        scratch_shapes=[pltpu.VMEM((CONV_HALO + tile, c), F32),
                        pltpu.VMEM((SUBLANES - 1, CONV_HALO + tile - SUBLANES, c), F32)],
        compiler_params=_cparams(("parallel", "parallel"), 32 * 2**20),
        name="conv_module",
    )(u, u, w, b_dw.reshape(1, c), gain.reshape(1, c), bias.reshape(1, c), bd)


def _chunk_cumsum(x):
    n = x.shape[0]
    row = lax.broadcasted_iota(jnp.int32, x.shape, 0)
    step = 1
    while step < n:
        x = x + jnp.where(row >= step, pltpu.roll(x, step, 0), 0.0)
        step *= 2
    return x


def _gla_body(q_ref, k_ref, v_ref, g_ref, r_ref, gain_ref, bd_ref, o_ref, st_ref, raw_ref, *, tile):
    @pl.when(pl.program_id(1) == 0)
    def _():
        st_ref[...] = jnp.zeros_like(st_ref)

    w = GLA_WIDTH
    ch = GLA_CHUNK
    log2_hd = HEAD_DIM.bit_length() - 1
    lane = lax.broadcasted_iota(jnp.int32, (ch, w), 1)
    row = lax.broadcasted_iota(jnp.int32, (ch, w), 0)
    head_of_lane = lax.shift_right_logical(lane, log2_hd)
    causal = (lane & (HEAD_DIM - 1)) <= row
    r_bd = lax.shift_right_logical(lax.broadcasted_iota(jnp.int32, (w, w), 0), log2_hd)
    c_bd = lax.shift_right_logical(lax.broadcasted_iota(jnp.int32, (w, w), 1), log2_hd)
    same_head = r_bd == c_bd
    scale = HEAD_DIM ** -0.5

    def stack_heads(t):
        return jnp.concatenate([jnp.where(head_of_lane == h, t, 0.0) for h in range(GLA_HEADS)], axis=0).astype(BF16)

    for c0 in range(0, tile, ch):
        rows = slice(c0, c0 + ch)
        q = q_ref[0, rows, :]
        k = k_ref[0, rows, :]
        v = v_ref[0, rows, :]
        cum = _chunk_cumsum(g_ref[0, rows, :])
        last = cum[ch - 1:ch, :]
        q_t = ((q * scale) * jnp.exp(cum)).astype(BF16)
        k_t = k * jnp.exp(-cum)
        k_end = (k * jnp.exp(last - cum)).astype(BF16)
        decay = jnp.exp(last)
        a = lax.dot_general(q_t, stack_heads(k_t), _NT, preferred_element_type=F32)
        a = jnp.where(causal, a, 0.0).astype(BF16)
        o_intra = jnp.dot(a, stack_heads(v), preferred_element_type=F32)
        st = st_ref[...]
        o_inter = lax.dot_general(q_t, st.astype(BF16), _NT, preferred_element_type=F32)
        kv_t = lax.dot_general(v.astype(BF16), k_end, _TN, preferred_element_type=F32)
        st_ref[...] = st * decay + jnp.where(same_head, kv_t, 0.0)
        raw_ref[rows, :] = o_intra + o_inter
    o = raw_ref[...]
    ss = _group_sum(o * o, bd_ref[...])
    y = (o * lax.rsqrt(ss * (1.0 / HEAD_DIM) + EPS)) * gain_ref[...]
    r = r_ref[0]
    o_ref[0] = (y * (r * jax.nn.sigmoid(r))).astype(BF16)


def _gla(q, k, v, g, r, out_gain):
    b, s, w = q.shape
    tile = min(SEQ_TILE, s)
    bd = _block_diag_ones(w, HEAD_DIM)
    seq = pl.BlockSpec((1, tile, w), lambda i, j: (i, j, 0))
    return pl.pallas_call(
        functools.partial(_gla_body, tile=tile),
        out_shape=jax.ShapeDtypeStruct((b, s, w), BF16),
        grid=(b, s // tile),
        in_specs=[seq, seq, seq, seq, seq,
                  pl.BlockSpec((1, w), lambda i, j: (0, 0)),
                  pl.BlockSpec((w, w), lambda i, j: (0, 0))],
        out_specs=seq,
        scratch_shapes=[pltpu.VMEM((w, w), F32), pltpu.VMEM((tile, w), F32)],
        compiler_params=_cparams(("parallel", "arbitrary"), 32 * 2**20),
        name="gla",
    )(q, k, v, g, r, jnp.tile(out_gain, GLA_HEADS).reshape(1, w), bd)


def kernel(x, c, positions, ada_w, ada_b, ffn1_norm, ffn1_w_in, ffn1_w_out, mix_norm, mix_w_in, q_norm, k_norm, conv_w, conv_b, conv_norm_g, conv_norm_b, gla_gate_w, gla_gate_b, gla_out_norm, mix_w_out, ffn2_norm, ffn2_w_in, ffn2_w_out):
    b, s, d = x.shape
    depth = ada_w.shape[0]
    n = b * s
    mod = _ada(c, ada_w, ada_b).reshape(depth, b, N_MOD, d)
    cos, sin = _rope_tables(positions)
    xf = x.reshape(n, d)
    for l in range(depth):
        xf = _ffn(xf, mod[l], ffn1_norm[l], _layer_bf16(ffn1_w_in, l), _layer_bf16(ffn1_w_out, l), mod_row=0,
                  rows_per_batch=s)
        wm, wg = _mix_in_weights(mix_w_in, l)
        q, k, v, u, gq, gk, gv, gr, gd = _proj(xf, mod[l], mix_norm[l], wm, wg, gla_gate_w[l], gla_gate_b[l],
                                               q_norm[l], k_norm[l], cos, sin, rows_per_batch=s)
        o_a = _attn(q.reshape(b, s, ATT_WIDTH), k.reshape(b, s, ATT_WIDTH), v.reshape(b, s, ATT_WIDTH))
        o_b = _conv(u.reshape(b, s, CONV_WIDTH), conv_w[l], conv_b[l], conv_norm_g[l], conv_norm_b[l])
        three = lambda t: t.reshape(b, s, GLA_WIDTH)
        o_c = _gla(three(gq), three(gk), three(gv), three(gd), three(gr), gla_out_norm[l])
        mix = (o_a.reshape(n, ATT_WIDTH), o_b.reshape(n, CONV_WIDTH), o_c.reshape(n, GLA_WIDTH),
               _layer_bf16(mix_w_out, l))
        xf = _ffn(xf, mod[l], ffn2_norm[l], _layer_bf16(ffn2_w_in, l), _layer_bf16(ffn2_w_out, l), mod_row=6,
                  rows_per_batch=s, mix=mix)
    return xf.reshape(b, s, d)
```

```python
import functools

import jax
import jax.numpy as jnp
from jax import lax
from jax.experimental import pallas as pl
from jax.experimental.pallas import tpu as pltpu

F32 = jnp.float32
BF16 = jnp.bfloat16

EPS = 1e-6
NEG_INF = -1e30
HEAD_DIM = 64
ATT_HEADS = 8
ATT_WIDTH = ATT_HEADS * HEAD_DIM
MOBA_BLOCK = 256
MOBA_TOPK = 3
ROPE_THETA = 10000.0
CONV_WIDTH = 256
CONV_GROUPS = 4
CONV_KERNEL = 31
GLA_HEADS = 4
GLA_WIDTH = 256
GLA_GATE_RANK = 16
GLA_TAU = 16.0
GLA_CHUNK = 64
FFN_RES = 0.5
N_MOD = 9

V7X_VMEM_BYTES = 64 * 2**20
LANES = 128
SUBLANES = 8
BF16_ROWS = 16
LOG2E = 1.4426950408889634
ROW_TILE = 512
PROJ_ROW_TILE = 1024
FF_CHUNK = 256
SEQ_TILE = 512
CONV_HALO = 32
CONV_SUB = 128

_NT = (((1,), (1,)), ((), ()))
_TN = (((0,), (0,)), ((), ()))


def _cparams(semantics, vmem_bytes, flags=None):
    assert vmem_bytes < V7X_VMEM_BYTES
    return pltpu.CompilerParams(dimension_semantics=semantics, vmem_limit_bytes=vmem_bytes, flags=flags)


def _resident(shape):
    zeros = (0,) * len(shape)
    return pl.BlockSpec(shape, lambda *_: zeros, pipeline_mode=pl.Buffered(1))


def _split_dot(x, w, terms):
    acc, r = None, x
    for t in range(terms):
        p = r.astype(BF16)
        d = jnp.dot(p, w, preferred_element_type=F32)
        acc = d if acc is None else acc + d
        if t + 1 < terms:
            r = r - p.astype(F32)
    return acc


def _group_sum(x, bd):
    return _split_dot(x, bd, 2)


def _block_diag_ones(n, group):
    idx = jnp.arange(n) // group
    return (idx[:, None] == idx[None, :]).astype(BF16)


def _mod_norm(x, gain, scale, shift):
    y = x * lax.rsqrt(jnp.mean(x * x, axis=-1, keepdims=True) + EPS)
    return (y * gain) * (1.0 + scale) + shift


def _cast_body(w_ref, o_ref):
    o_ref[...] = w_ref[0].astype(BF16)


def _layer_bf16(w, layer):
    _, rows, cols = w.shape
    steps = 4
    assert rows % (steps * BF16_ROWS) == 0
    tr = rows // steps
    return pl.pallas_call(
        _cast_body,
        out_shape=jax.ShapeDtypeStruct((rows, cols), BF16),
        grid=(steps,),
        in_specs=[pl.BlockSpec((1, tr, cols), lambda i: (layer, i, 0))],
        out_specs=pl.BlockSpec((tr, cols), lambda i: (i, 0)),
        compiler_params=_cparams(("parallel",), 40 * 2**20),
        name="cast_bf16",
    )(w)


def _ada_body(c_ref, w_ref, b_ref, o_ref):
    c = c_ref[...]
    c_act = (c * jax.nn.sigmoid(c)).astype(BF16)
    o_ref[0] = jnp.dot(c_act, w_ref[0].astype(BF16), preferred_element_type=F32) + b_ref[0]


def _ada(c, ada_w, ada_b):
    depth, d, m = ada_w.shape
    b = c.shape[0]
    tn = 2304
    assert m % tn == 0
    return pl.pallas_call(
        _ada_body,
        out_shape=jax.ShapeDtypeStruct((depth, b, m), F32),
        grid=(depth, m // tn),
        in_specs=[pl.BlockSpec((b, d), lambda l, j: (0, 0)),
                  pl.BlockSpec((1, d, tn), lambda l, j: (l, 0, j)),
                  pl.BlockSpec((1, 1, tn), lambda l, j: (l, 0, j))],
        out_specs=pl.BlockSpec((1, b, tn), lambda l, j: (l, 0, j)),
        compiler_params=_cparams(("parallel", "parallel"), 32 * 2**20),
        name="ada_mod",
    )(c, ada_w, ada_b.reshape(depth, 1, m))


def _rope_body(pos_ref, invf_ref, sign_ref, cos_ref, sin_ref):
    ang = pos_ref[0].astype(F32) * invf_ref[...]
    cos_ref[0] = jnp.cos(ang)
    sin_ref[0] = jnp.sin(ang) * sign_ref[...]


def _rope_tables(positions):
    b, s = positions.shape
    half = HEAD_DIM // 2
    inv_freq = 1.0 / (ROPE_THETA ** (jnp.arange(0, HEAD_DIM, 2, dtype=F32) / HEAD_DIM))
    invf = jnp.tile(inv_freq, LANES // half).reshape(1, LANES)
    sign = jnp.tile(jnp.concatenate([-jnp.ones((half,), F32), jnp.ones((half,), F32)]), LANES // HEAD_DIM).reshape(1, LANES)
    ts = min(s, 1024)
    shape = jax.ShapeDtypeStruct((b, s, LANES), F32)
    return pl.pallas_call(
        _rope_body,
        out_shape=(shape, shape),
        grid=(b, s // ts),
        in_specs=[pl.BlockSpec((1, ts, 1), lambda i, j: (i, j, 0)),
                  pl.BlockSpec((1, LANES), lambda i, j: (0, 0)),
                  pl.BlockSpec((1, LANES), lambda i, j: (0, 0))],
        out_specs=(pl.BlockSpec((1, ts, LANES), lambda i, j: (i, j, 0)),
                   pl.BlockSpec((1, ts, LANES), lambda i, j: (i, j, 0))),
        compiler_params=_cparams(("parallel", "parallel"), 32 * 2**20),
        name="rope_tables",
    )(positions.reshape(b, s, 1), invf, sign)


def _ffn_body(*refs, mod_row, d_ff, with_mix):
    if with_mix:
        x_ref, mod_ref, gain_ref, wi_ref, wo_ref, oa_ref, ob_ref, oc_ref, wm_ref, o_ref, hb_ref, acc_ref = refs
        wa = ATT_WIDTH
        wb = wa + CONV_WIDTH
        mixed = jnp.dot(oa_ref[...], wm_ref[0:wa, :], preferred_element_type=F32)
        mixed = mixed + jnp.dot(ob_ref[...], wm_ref[wa:wb, :], preferred_element_type=F32)
        mixed = mixed + jnp.dot(oc_ref[...], wm_ref[wb:wb + GLA_WIDTH, :], preferred_element_type=F32)
        x = x_ref[...] + mod_ref[0, 5:6, :] * mixed
    else:
        x_ref, mod_ref, gain_ref, wi_ref, wo_ref, o_ref, hb_ref, acc_ref = refs
        x = x_ref[...]
    shift = mod_ref[0, mod_row:mod_row + 1, :]
    scale = mod_ref[0, mod_row + 1:mod_row + 2, :]
    gate = mod_ref[0, mod_row + 2:mod_row + 3, :]
    hb_ref[...] = _mod_norm(x, gain_ref[...], scale, shift).astype(BF16)
    for c0 in range(0, d_ff, FF_CHUNK):
        a = jnp.dot(hb_ref[...], wi_ref[:, c0:c0 + FF_CHUNK], preferred_element_type=F32)
        b = jnp.dot(hb_ref[...], wi_ref[:, d_ff + c0:d_ff + c0 + FF_CHUNK], preferred_element_type=F32)
        g = (a * jax.nn.sigmoid(a) * b).astype(BF16)
        part = jnp.dot(g, wo_ref[c0:c0 + FF_CHUNK, :], preferred_element_type=F32)
        if c0 == 0:
            acc_ref[...] = part
        else:
            acc_ref[...] += part
    o_ref[...] = x + (FFN_RES * gate) * acc_ref[...]


def _ffn(x, mod, gain, wi, wo, *, mod_row, rows_per_batch, mix=None):
    n, d = x.shape
    d_ff = wo.shape[0]
    assert d_ff % FF_CHUNK == 0
    tm = min(ROW_TILE, rows_per_batch)
    tiles_per_batch = rows_per_batch // tm
    row = lambda i: (i, 0)
    args = [x, mod, gain.reshape(1, d), wi, wo]
    in_specs = [pl.BlockSpec((tm, d), row),
                pl.BlockSpec((1, N_MOD, d), lambda i: (i // tiles_per_batch, 0, 0)),
                pl.BlockSpec((1, d), lambda i: (0, 0)),
                _resident(wi.shape),
                _resident(wo.shape)]
    if mix is not None:
        o_a, o_b, o_c, wm = mix
        args += [o_a, o_b, o_c, wm]
        in_specs += [pl.BlockSpec((tm, ATT_WIDTH), row), pl.BlockSpec((tm, CONV_WIDTH), row),
                     pl.BlockSpec((tm, GLA_WIDTH), row), _resident(wm.shape)]
    body = functools.partial(_ffn_body, mod_row=mod_row, d_ff=d_ff, with_mix=mix is not None)
    return pl.pallas_call(
        body,
        out_shape=jax.ShapeDtypeStruct((n, d), F32),
        grid=(n // tm,),
        in_specs=in_specs,
        out_specs=pl.BlockSpec((tm, d), row),
        scratch_shapes=[pltpu.VMEM((tm, d), BF16), pltpu.VMEM((tm, d), F32)],
        compiler_params=_cparams(("parallel",), 48 * 2**20),
        name="ffn_mix" if mix is not None else "ffn",
    )(*args)


def _log_sigmoid(z):
    return jnp.minimum(z, 0.0) - jnp.log1p(jnp.exp(-jnp.abs(z)))


def _proj_body(x_ref, mod_ref, gain_ref, wm_ref, wg_ref, gw_ref, gb_ref, qg_ref, kg_ref, cos_ref, sin_ref, bd_ref,
               q_ref, k_ref, v_ref, u_ref, gq_ref, gk_ref, gv_ref, gr_ref, gd_ref, hb_ref):
    x = x_ref[...]
    hb_ref[...] = _mod_norm(x, gain_ref[...], mod_ref[0, 4:5, :], mod_ref[0, 3:4, :]).astype(BF16)

    def proj(lo, hi):
        return jnp.dot(hb_ref[...], wm_ref[0, :, lo:hi], preferred_element_type=F32)

    reps = ATT_WIDTH // LANES
    cos = jnp.concatenate([cos_ref[0]] * reps, axis=1)
    sin = jnp.concatenate([sin_ref[0]] * reps, axis=1)
    lane = lax.broadcasted_iota(jnp.int32, cos.shape, 1)
    first_half = (lane & (HEAD_DIM // 2)) == 0

    def norm_rope(t, g_ref):
        sq = t * t
        half = ATT_WIDTH // 2
        ss = jnp.concatenate([_group_sum(sq[:, :half], bd_ref[...]), _group_sum(sq[:, half:], bd_ref[...])], axis=1)
        y = (t * lax.rsqrt(ss * (1.0 / HEAD_DIM) + EPS)) * g_ref[...]
        partner = jnp.where(first_half, pltpu.roll(y, ATT_WIDTH - HEAD_DIM // 2, 1), pltpu.roll(y, HEAD_DIM // 2, 1))
        return y * cos + partner * sin

    w = ATT_WIDTH
    q_ref[...] = norm_rope(proj(0, w), qg_ref).astype(BF16)
    k_ref[...] = norm_rope(proj(w, 2 * w), kg_ref).astype(BF16)
    v_ref[...] = proj(2 * w, 3 * w).astype(BF16)
    o = 3 * w
    ug = proj(o, o + 2 * CONV_WIDTH)
    u_ref[...] = ug[:, :CONV_WIDTH] * jax.nn.sigmoid(ug[:, CONV_WIDTH:])
    o += 2 * CONV_WIDTH
    gq_ref[...] = proj(o, o + GLA_WIDTH)
    gk_ref[...] = proj(o + GLA_WIDTH, o + 2 * GLA_WIDTH)
    gv_ref[...] = proj(o + 2 * GLA_WIDTH, o + 3 * GLA_WIDTH)
    gr_ref[...] = proj(o + 3 * GLA_WIDTH, o + 4 * GLA_WIDTH)
    cg = jnp.dot(hb_ref[...], wg_ref[0], preferred_element_type=F32).astype(BF16)
    z = jnp.dot(cg, gw_ref[...], preferred_element_type=F32) + gb_ref[...]
    gd_ref[...] = _log_sigmoid(z) * (1.0 / GLA_TAU)


def _mix_in_weights(w_in):
    main = 3 * ATT_WIDTH + 2 * CONV_WIDTH + 4 * GLA_WIDTH
    assert w_in.shape[2] == main + GLA_GATE_RANK
    wm = w_in[:, :, :main].astype(BF16)
    wg = jnp.pad(w_in[:, :, main:], ((0, 0), (0, 0), (0, LANES - GLA_GATE_RANK))).astype(BF16)
    return wm, wg


def _proj(x, mod, gain, wm, wg, layer, gate_w, gate_b, q_gain, k_gain, cos, sin, *, rows_per_batch):
    n, d = x.shape
    gw = jnp.pad(gate_w, ((0, LANES - GLA_GATE_RANK), (0, 0))).astype(BF16)
    bd = _block_diag_ones(ATT_WIDTH // 2, HEAD_DIM)
    layer_block = lambda a: pl.BlockSpec((1,) + a.shape[1:], lambda i: (layer, 0, 0), pipeline_mode=pl.Buffered(1))
    tm = min(PROJ_ROW_TILE, rows_per_batch)
    tpb = rows_per_batch // tm
    row = lambda i: (i, 0)
    const = lambda i: (0, 0)
    tab = pl.BlockSpec((1, tm, LANES), lambda i: (i // tpb, i % tpb, 0))
    att = jax.ShapeDtypeStruct((n, ATT_WIDTH), BF16)
    g32 = jax.ShapeDtypeStruct((n, GLA_WIDTH), F32)
    return pl.pallas_call(
        _proj_body,
        out_shape=(att, att, att, jax.ShapeDtypeStruct((n, CONV_WIDTH), F32), g32, g32, g32, g32, g32),
        grid=(n // tm,),
        in_specs=[pl.BlockSpec((tm, d), row),
                  pl.BlockSpec((1, N_MOD, d), lambda i: (i // tpb, 0, 0)),
                  pl.BlockSpec((1, d), const),
                  layer_block(wm), layer_block(wg), _resident(gw.shape),
                  pl.BlockSpec((1, GLA_WIDTH), const),
                  pl.BlockSpec((1, ATT_WIDTH), const), pl.BlockSpec((1, ATT_WIDTH), const),
                  tab, tab, _resident(bd.shape)],
        out_specs=(pl.BlockSpec((tm, ATT_WIDTH), row),) * 3 + (pl.BlockSpec((tm, CONV_WIDTH), row),)
        + (pl.BlockSpec((tm, GLA_WIDTH), row),) * 5,
        scratch_shapes=[pltpu.VMEM((tm, d), BF16)],
        compiler_params=_cparams(("parallel",), 48 * 2**20),
        name="mix_proj",
    )(x, mod, gain.reshape(1, d), wm, wg, gw, gate_b.reshape(1, GLA_WIDTH),
      jnp.tile(q_gain * LOG2E, ATT_HEADS).reshape(1, ATT_WIDTH), jnp.tile(k_gain, ATT_HEADS).reshape(1, ATT_WIDTH),
      cos, sin, bd)


def _attn_body(q_ref, k_ref, v_ref, o_ref, vt_ref, b0_ref, b1_ref, s0_ref, s1_ref, qq_ref, ot_ref, *, n_blk):
    blk = MOBA_BLOCK
    heads = LANES // HEAD_DIM
    lane = lax.broadcasted_iota(jnp.int32, (1, LANES), 1)
    log2_hd = HEAD_DIM.bit_length() - 1

    for j in range(n_blk):
        vt_ref[j] = v_ref[0, j * blk:(j + 1) * blk, :].astype(F32).T.astype(BF16)

    kmean = jnp.concatenate(
        [jnp.mean(k_ref[0, j * blk:(j + 1) * blk, :].astype(F32), axis=0, keepdims=True) for j in range(n_blk)], axis=0)

    jrow = lax.broadcasted_iota(jnp.int32, (n_blk, blk), 0)
    kmean_h = [jnp.where(lax.shift_right_logical(lane, log2_hd) == h, kmean, 0.0).astype(BF16) for h in range(heads)]

    def selection_bias(h, q_rows, i):
        g = lax.dot_general(kmean_h[h], q_rows, _NT, preferred_element_type=F32)
        past = jrow < i
        g = jnp.where(past, g, NEG_INF)
        rank = jnp.zeros((n_blk, blk), F32)
        for jp in range(n_blk):
            gj = g[jp:jp + 1, :]
            rank = rank + jnp.where(jrow > jp, jnp.where(gj >= g, 1.0, 0.0), jnp.where(gj > g, 1.0, 0.0))
        sel = jnp.logical_and(rank < float(min(MOBA_TOPK, n_blk)), past)
        bias = jnp.where(sel, 0.0, NEG_INF)
        return jnp.broadcast_to(bias[:, None, :], (n_blk, SUBLANES, blk)).reshape(n_blk * SUBLANES, blk)

    key_row = lax.broadcasted_iota(jnp.int32, (blk, blk), 0)
    qry_col = lax.broadcasted_iota(jnp.int32, (blk, blk), 1)
    causal = key_row <= qry_col
    scale = HEAD_DIM ** -0.5
    n_past = n_blk - 1
    n_shared = n_blk // 2 - 1
    zero_v = jnp.zeros((HEAD_DIM, blk), BF16)
    sum_row = lax.broadcasted_iota(jnp.int32, (BF16_ROWS, blk), 0)
    sum_a = jnp.where(sum_row == 0, 1.0, 0.0).astype(BF16)
    sum_b = jnp.where(sum_row == 1, 1.0, 0.0).astype(BF16)

    def k_blk(j):
        return k_ref[0, pl.ds(pl.multiple_of(j * blk, blk), blk), :]

    def tile_max(s):
        return jnp.max(s.reshape(blk // SUBLANES, SUBLANES, blk), axis=0)

    def score_pass(t, s_ref, b_ref):
        ia = t
        ib = n_past - t
        q_a = q_ref[0, pl.ds(pl.multiple_of(ia * blk, blk), blk), :]
        q_b = q_ref[0, pl.ds(pl.multiple_of(ib * blk, blk), blk), :]
        maxes = []
        for h in range(heads):
            in_head = lax.shift_right_logical(lane, log2_hd) == h

            def head_q(qi, in_head=in_head):
                return (jnp.where(in_head, qi, jnp.zeros_like(qi)).astype(F32) * scale).astype(BF16)

            b_ref[h, 0] = selection_bias(h, q_a, ia)
            b_ref[h, 1] = selection_bias(h, q_b, ib)
            qa = head_q(q_a)
            qb = head_q(q_b)
            qq_ref[h, 0] = qa
            qq_ref[h, 1] = qb
            sa = jnp.where(causal, lax.dot_general(k_blk(ia), qa, _NT, preferred_element_type=F32), NEG_INF)
            sb = jnp.where(causal, lax.dot_general(k_blk(ib), qb, _NT, preferred_element_type=F32), NEG_INF)
            s_ref[h, 0] = sa
            s_ref[h, 1] = sb
            m8a = tile_max(sa)
            m8b = tile_max(sb)
            for m in range(n_past):
                shared = m < n_shared
                is_a = m < t if shared else False
                j = jnp.where(is_a, m, m - t) if shared else m - t
                which = jnp.where(is_a, 0, 1) if shared else 1
                s = lax.dot_general(k_blk(j), qq_ref[h, which], _NT, preferred_element_type=F32)
                b8 = b_ref[h, which, pl.ds(pl.multiple_of(j * SUBLANES, SUBLANES), SUBLANES), :]
                s3 = s.reshape(blk // SUBLANES, SUBLANES, blk) + b8[None]
                s_ref[h, 2 + m] = s3.reshape(blk, blk)
                mx = jnp.max(s3, axis=0)
                if shared:
                    m8a = jnp.where(is_a, jnp.maximum(m8a, mx), m8a)
                    m8b = jnp.where(is_a, m8b, jnp.maximum(m8b, mx))
                else:
                    m8b = jnp.maximum(m8b, mx)
            maxes.append(jnp.max(m8a, axis=0, keepdims=True))
            maxes.append(jnp.max(m8b, axis=0, keepdims=True))
        return tuple(maxes)

    def weighted_pass(t, maxes, s_ref):
        ia = t
        ib = n_past - t
        for h in range(heads):
            v_rows = slice(h * HEAD_DIM, (h + 1) * HEAD_DIM)
            ma = maxes[2 * h]
            mb = maxes[2 * h + 1]

            def vt_blk(j, v_rows=v_rows):
                return vt_ref[j, v_rows, :]

            def weigh(j, scores, mx):
                p = jnp.exp2(scores - mx).astype(BF16)
                return jnp.dot(jnp.concatenate([vt_blk(j), sum_a], axis=0), p, preferred_element_type=F32)

            acc_a = weigh(ia, s_ref[h, 0], ma)
            acc_b = weigh(ib, s_ref[h, 1], mb)
            for m in range(n_shared, n_past):
                acc_b = acc_b + weigh(m - t, s_ref[h, 2 + m], mb)
            acc = jnp.zeros((2 * HEAD_DIM + BF16_ROWS, blk), F32)
            for m in range(n_shared):
                is_a = m < t
                j = jnp.where(is_a, m, m - t)
                p = jnp.exp2(s_ref[h, 2 + m] - jnp.where(is_a, ma, mb)).astype(BF16)
                vt = vt_blk(j)
                lhs = jnp.concatenate([jnp.where(is_a, vt, zero_v), jnp.where(is_a, zero_v, vt),
                                       jnp.where(is_a, sum_a, sum_b)], axis=0)
                acc = acc + jnp.dot(lhs, p, preferred_element_type=F32)
            hd = HEAD_DIM
            den = 2 * hd
            ot_ref[ia, v_rows, :] = (acc_a[:hd] + acc[:hd]) / (acc_a[hd:hd + 1] + acc[den:den + 1])
            ot_ref[ib, v_rows, :] = (acc_b[:hd] + acc[hd:den]) / (acc_b[hd:hd + 1] + acc[den + 1:den + 2])

    def two_pairs(u, maxes):
        t = 2 * u
        mid = score_pass(t + 1, s1_ref, b1_ref)
        weighted_pass(t, maxes, s0_ref)
        nxt = score_pass(t + 2, s0_ref, b0_ref)
        weighted_pass(t + 1, mid, s1_ref)
        return nxt

    n_pairs = n_blk // 2
    maxes = lax.fori_loop(0, n_pairs // 2 - 1, two_pairs, score_pass(0, s0_ref, b0_ref))
    mid = score_pass(n_pairs - 1, s1_ref, b1_ref)
    weighted_pass(n_pairs - 2, maxes, s0_ref)
    weighted_pass(n_pairs - 1, mid, s1_ref)

    for i in range(n_blk):
        o_ref[0, i * blk:(i + 1) * blk, :] = ot_ref[i].T.astype(BF16)


def _attn(q, k, v):
    b, s, w = q.shape
    assert s % (4 * MOBA_BLOCK) == 0
    n_blk = s // MOBA_BLOCK
    heads = LANES // HEAD_DIM
    spec = pl.BlockSpec((1, s, LANES), lambda i, p: (i, 0, p))
    return pl.pallas_call(
        functools.partial(_attn_body, n_blk=n_blk),
        out_shape=jax.ShapeDtypeStruct((b, s, w), BF16),
        grid=(b, w // LANES),
        in_specs=[spec, spec, spec],
        out_specs=spec,
        scratch_shapes=[pltpu.VMEM((n_blk, LANES, MOBA_BLOCK), BF16),
                        pltpu.VMEM((heads, 2, n_blk * SUBLANES, MOBA_BLOCK), F32),
                        pltpu.VMEM((heads, 2, n_blk * SUBLANES, MOBA_BLOCK), F32),
                        pltpu.VMEM((heads, n_blk + 1, MOBA_BLOCK, MOBA_BLOCK), F32),
                        pltpu.VMEM((heads, n_blk + 1, MOBA_BLOCK, MOBA_BLOCK), F32),
                        pltpu.VMEM((heads, 2, MOBA_BLOCK, LANES), BF16),
                        pltpu.VMEM((n_blk, LANES, MOBA_BLOCK), F32)],
        compiler_params=_cparams(("parallel", "parallel"), 48 * 2**20),
        name="moba_attn",
    )(q, k, v)


def _conv_body(cur_ref, prev_ref, w_ref, cb_ref, g_ref, b_ref, bd_ref, o_ref, win_ref, sh_ref, *, tile):
    halo = prev_ref[0, tile - CONV_HALO:tile, :]
    win_ref[0:CONV_HALO] = jnp.where(pl.program_id(1) > 0, halo, 0.0)
    win_ref[CONV_HALO:CONV_HALO + tile] = cur_ref[0]
    n_sh = CONV_HALO + tile - SUBLANES
    for r in range(1, SUBLANES):
        sh_ref[r - 1] = win_ref[r:r + n_sh, :]
    lead = CONV_HALO - (CONV_KERNEL - 1)
    inv = 1.0 / (CONV_WIDTH // CONV_GROUPS)
    for r0 in range(0, tile, CONV_SUB):
        acc = jnp.broadcast_to(cb_ref[...], (CONV_SUB, CONV_WIDTH))
        for t in range(CONV_KERNEL):
            base, phase = divmod(lead + t, SUBLANES)
            lo = r0 + base * SUBLANES
            x = win_ref[lo:lo + CONV_SUB, :] if phase == 0 else sh_ref[phase - 1, lo:lo + CONV_SUB, :]
            acc = acc + w_ref[t:t + 1, :] * x
        mu = _group_sum(acc, bd_ref[...]) * inv
        xc = acc - mu
        var = _group_sum(xc * xc, bd_ref[...]) * inv
        y = (xc * lax.rsqrt(var + EPS)) * g_ref[...] + b_ref[...]
        o_ref[0, r0:r0 + CONV_SUB, :] = (y * jax.nn.sigmoid(y)).astype(BF16)


def _conv(u, w_dw, b_dw, gain, bias):
    b, s, c = u.shape
    tile = min(SEQ_TILE, s)
    w = jnp.pad(w_dw, ((0, CONV_HALO - CONV_KERNEL), (0, 0)))
    bd = _block_diag_ones(c, c // CONV_GROUPS)
    vec = pl.BlockSpec((1, c), lambda i, j: (0, 0))
    return pl.pallas_call(
        functools.partial(_conv_body, tile=tile),
        out_shape=jax.ShapeDtypeStruct((b, s, c), BF16),
        grid=(b, s // tile),
        in_specs=[pl.BlockSpec((1, tile, c), lambda i, j: (i, j, 0)),
                  pl.BlockSpec((1, tile, c), lambda i, j: (i, jnp.maximum(j - 1, 0), 0)),
                  pl.BlockSpec((CONV_HALO, c), lambda i, j: (0, 0)),
                  vec, vec, vec, pl.BlockSpec((c, c), lambda i, j: (0, 0))],
        out_specs=pl.BlockSpec((1, tile, c), lambda i, j: (i, j, 0)),
        scratch_shapes=[pltpu.VMEM((CONV_HALO + tile, c), F32),
                        pltpu.VMEM((SUBLANES - 1, CONV_HALO + tile - SUBLANES, c), F32)],
        compiler_params=_cparams(("parallel", "parallel"), 32 * 2**20),
        name="conv_module",
    )(u, u, w, b_dw.reshape(1, c), gain.reshape(1, c), bias.reshape(1, c), bd)


def _chunk_cumsum(x):
    n = x.shape[0]
    row = lax.broadcasted_iota(jnp.int32, x.shape, 0)
    step = 1
    while step < n:
        x = x + jnp.where(row >= step, pltpu.roll(x, step, 0), 0.0)
        step *= 2
    return x


def _gla_body(q_ref, k_ref, v_ref, g_ref, r_ref, gain_ref, bd_ref, o_ref, st_ref, raw_ref, *, tile):
    @pl.when(pl.program_id(1) == 0)
    def _():
        st_ref[...] = jnp.zeros_like(st_ref)

    w = GLA_WIDTH
    ch = GLA_CHUNK
    log2_hd = HEAD_DIM.bit_length() - 1
    lane = lax.broadcasted_iota(jnp.int32, (ch, w), 1)
    row = lax.broadcasted_iota(jnp.int32, (ch, w), 0)
    head_of_lane = lax.shift_right_logical(lane, log2_hd)
    causal = (lane & (HEAD_DIM - 1)) <= row
    r_bd = lax.shift_right_logical(lax.broadcasted_iota(jnp.int32, (w, w), 0), log2_hd)
    c_bd = lax.shift_right_logical(lax.broadcasted_iota(jnp.int32, (w, w), 1), log2_hd)
    same_head = r_bd == c_bd
    scale = HEAD_DIM ** -0.5

    def stack_heads(t):
        return jnp.concatenate([jnp.where(head_of_lane == h, t, 0.0) for h in range(GLA_HEADS)], axis=0).astype(BF16)

    for c0 in range(0, tile, ch):
        rows = slice(c0, c0 + ch)
        q = q_ref[0, rows, :]
        k = k_ref[0, rows, :]
        v = v_ref[0, rows, :]
        cum = _chunk_cumsum(g_ref[0, rows, :])
        last = cum[ch - 1:ch, :]
        q_t = ((q * scale) * jnp.exp(cum)).astype(BF16)
        k_t = k * jnp.exp(-cum)
        k_end = (k * jnp.exp(last - cum)).astype(BF16)
        decay = jnp.exp(last)
        a = lax.dot_general(q_t, stack_heads(k_t), _NT, preferred_element_type=F32)
        a = jnp.where(causal, a, 0.0).astype(BF16)
        o_intra = jnp.dot(a, stack_heads(v), preferred_element_type=F32)
        st = st_ref[...]
        o_inter = lax.dot_general(q_t, st.astype(BF16), _NT, preferred_element_type=F32)
        kv_t = lax.dot_general(v.astype(BF16), k_end, _TN, preferred_element_type=F32)
        st_ref[...] = st * decay + jnp.where(same_head, kv_t, 0.0)
        raw_ref[rows, :] = o_intra + o_inter
    o = raw_ref[...]
    ss = _group_sum(o * o, bd_ref[...])
    y = (o * lax.rsqrt(ss * (1.0 / HEAD_DIM) + EPS)) * gain_ref[...]
    r = r_ref[0]
    o_ref[0] = (y * (r * jax.nn.sigmoid(r))).astype(BF16)


def _gla(q, k, v, g, r, out_gain):
    b, s, w = q.shape
    tile = min(SEQ_TILE, s)
    bd = _block_diag_ones(w, HEAD_DIM)
    seq = pl.BlockSpec((1, tile, w), lambda i, j: (i, j, 0))
    return pl.pallas_call(
        functools.partial(_gla_body, tile=tile),
        out_shape=jax.ShapeDtypeStruct((b, s, w), BF16),
        grid=(b, s // tile),
        in_specs=[seq, seq, seq, seq, seq,
                  pl.BlockSpec((1, w), lambda i, j: (0, 0)),
                  pl.BlockSpec((w, w), lambda i, j: (0, 0))],
        out_specs=seq,
        scratch_shapes=[pltpu.VMEM((w, w), F32), pltpu.VMEM((tile, w), F32)],
        compiler_params=_cparams(("parallel", "arbitrary"), 32 * 2**20),
        name="gla",
    )(q, k, v, g, r, jnp.tile(out_gain, GLA_HEADS).reshape(1, w), bd)


def kernel(x, c, positions, ada_w, ada_b, ffn1_norm, ffn1_w_in, ffn1_w_out, mix_norm, mix_w_in, q_norm, k_norm, conv_w, conv_b, conv_norm_g, conv_norm_b, gla_gate_w, gla_gate_b, gla_out_norm, mix_w_out, ffn2_norm, ffn2_w_in, ffn2_w_out):
    b, s, d = x.shape
    depth = ada_w.shape[0]
    n = b * s
    mod = _ada(c, ada_w, ada_b).reshape(depth, b, N_MOD, d)
    cos, sin = _rope_tables(positions)
    xf = x.reshape(n, d)
    wm, wg = _mix_in_weights(mix_w_in)
    for l in range(depth):
        xf = _ffn(xf, mod[l], ffn1_norm[l], _layer_bf16(ffn1_w_in, l), _layer_bf16(ffn1_w_out, l), mod_row=0,
                  rows_per_batch=s)
        q, k, v, u, gq, gk, gv, gr, gd = _proj(xf, mod[l], mix_norm[l], wm, wg, l, gla_gate_w[l], gla_gate_b[l],
                                               q_norm[l], k_norm[l], cos, sin, rows_per_batch=s)
        o_a = _attn(q.reshape(b, s, ATT_WIDTH), k.reshape(b, s, ATT_WIDTH), v.reshape(b, s, ATT_WIDTH))
        o_b = _conv(u.reshape(b, s, CONV_WIDTH), conv_w[l], conv_b[l], conv_norm_g[l], conv_norm_b[l])
        three = lambda t: t.reshape(b, s, GLA_WIDTH)
        o_c = _gla(three(gq), three(gk), three(gv), three(gd), three(gr), gla_out_norm[l])
        mix = (o_a.reshape(n, ATT_WIDTH), o_b.reshape(n, CONV_WIDTH), o_c.reshape(n, GLA_WIDTH),
               _layer_bf16(mix_w_out, l))
        xf = _ffn(xf, mod[l], ffn2_norm[l], _layer_bf16(ffn2_w_in, l), _layer_bf16(ffn2_w_out, l), mod_row=6,
                  rows_per_batch=s, mix=mix)
    return xf.reshape(b, s, d)
```
